```python
import jax
import jax.numpy as jnp
from jax import lax
import numpy as np

D_MODEL = 1024
BATCH = 1
SEQ = 16384
DEPTH = 1
DEC_BATCH = 128
DEC_SEQ = 1
PAST_LEN = 8192
PAGE_SIZE = 128

N_META = 16
D_CONV = 512
CONV_W = 31
N_HEADS = 8
HEAD_DIM = 64
ATT_W = N_HEADS * HEAD_DIM
BRANCH_W = 512
N_IDX_HEADS = 8
D_IDX = 64
MAX_TOPK = 256
D_FF = 2816
Q_BLOCK = 128
ALPHA = (2 * DEPTH) ** 0.25
BETA = (8 * DEPTH) ** -0.25
LN_EPS = 1e-5

C_GLU = 0
C_Q = C_GLU + 2 * D_CONV
C_K = C_Q + ATT_W
C_V = C_K + ATT_W
C_QI = C_V + ATT_W
C_KI = C_QI + N_IDX_HEADS * D_IDX
C_WI = C_KI + D_IDX
C_G = C_WI + N_IDX_HEADS
N_IN_COLS = C_G + 2 * D_MODEL

kernel_name = "meta_conformer_dsa_hybrid_step"


def _layer_norm(x, g, b):
    xf = x.astype(jnp.float32)
    mu = jnp.mean(xf, axis=-1, keepdims=True)
    var = jnp.mean(jnp.square(xf - mu), axis=-1, keepdims=True)
    y = (xf - mu) * lax.rsqrt(var + LN_EPS) * g.astype(jnp.float32) + b.astype(jnp.float32)
    return y.astype(x.dtype)


def _swiglu(x, w_in, w_out):
    gate, up = jnp.split(x @ w_in, 2, axis=-1)
    return (jax.nn.silu(gate) * up) @ w_out


def _pre_mixer(x, ffn1_w_in, ffn1_w_out, ln1_g, ln1_b, w_in):
    x1 = _layer_norm(ALPHA * x + 0.5 * _swiglu(x, ffn1_w_in, ffn1_w_out), ln1_g, ln1_b)
    return x1, x1 @ w_in


def _post_mixer(x1, h, ln2_g, ln2_b, ffn2_w_in, ffn2_w_out, ln3_g, ln3_b):
    x2 = _layer_norm(ALPHA * x1 + h, ln2_g, ln2_b)
    return _layer_norm(ALPHA * x2 + 0.5 * _swiglu(x2, ffn2_w_in, ffn2_w_out), ln3_g, ln3_b)


def _split_cols(z):
    lead = z.shape[:-1]
    u = z[..., C_GLU:C_GLU + D_CONV] * jax.nn.sigmoid(z[..., C_GLU + D_CONV:C_Q])
    q = z[..., C_Q:C_K].reshape(lead + (N_HEADS, HEAD_DIM))
    k = z[..., C_K:C_V].reshape(lead + (N_HEADS, HEAD_DIM))
    v = z[..., C_V:C_QI].reshape(lead + (N_HEADS, HEAD_DIM))
    qi = z[..., C_QI:C_KI].reshape(lead + (N_IDX_HEADS, D_IDX))
    ki = z[..., C_KI:C_WI]
    wi = z[..., C_WI:C_G]
    gates = z[..., C_G:]
    return u, q, k, v, qi, ki, wi, gates


def _conv_module(u_ext, conv_w, conv_b, conv_ln_g, conv_ln_b):
    c = lax.conv_general_dilated(
        u_ext, conv_w[:, None, :].astype(u_ext.dtype), window_strides=(1,), padding="VALID",
        dimension_numbers=("NWC", "WIO", "NWC"), feature_group_count=D_CONV)
    c = c + conv_b
    return jax.nn.silu(_layer_norm(c, conv_ln_g, conv_ln_b))


def _index_scores(qi, wi, ki):
    s = jnp.einsum("bthd,bsd->bths", qi, ki, preferred_element_type=jnp.float32)
    return jnp.einsum("bths,bth->bts", jax.nn.relu(s), wi.astype(jnp.float32))


def _attend_selected(q, k_sel, v_sel, valid):
    s = jnp.einsum("bthd,btkhd->bthk", q, k_sel, preferred_element_type=jnp.float32)
    s = s * (HEAD_DIM ** -0.5)
    s = jnp.where(valid[:, :, None, :], s, -jnp.inf)
    p = jax.nn.softmax(s, axis=-1)
    return jnp.einsum("bthk,btkhd->bthd", p.astype(v_sel.dtype), v_sel)


def _gather_rows(rows, idx):
    return jax.vmap(lambda r, i: r[i])(rows, idx)


def _merge(u_conv, att, gates, b_gate, w_branch, w_out):
    g_conv, g_att = jnp.split(gates + b_gate, 2, axis=-1)
    m = jax.nn.sigmoid(g_conv) * (u_conv @ w_branch[0]) + jax.nn.sigmoid(g_att) * (att @ w_branch[1])
    return m @ w_out


def _pad_seq(a, n):
    return jnp.pad(a, [(0, 0), (0, n)] + [(0, 0)] * (a.ndim - 2))


def _prompt_sparse_attention(q, k, v, qi, ki, wi):
    b, l = q.shape[:2]
    lp = -(-l // Q_BLOCK) * Q_BLOCK
    pad = lp - l
    q_p, k_p, v_p = _pad_seq(q, pad), _pad_seq(k, pad), _pad_seq(v, pad)
    qi_p, ki_p, wi_p = _pad_seq(qi, pad), _pad_seq(ki, pad), _pad_seq(wi, pad)
    topk = min(MAX_TOPK, SEQ // 4)
    key_pos = jnp.arange(lp)

    def block(i):
        t0 = i * Q_BLOCK
        qb = lax.dynamic_slice_in_dim(q_p, t0, Q_BLOCK, axis=1)
        qib = lax.dynamic_slice_in_dim(qi_p, t0, Q_BLOCK, axis=1)
        wib = lax.dynamic_slice_in_dim(wi_p, t0, Q_BLOCK, axis=1)
        tpos = t0 + jnp.arange(Q_BLOCK)
        sc = _index_scores(qib, wib, ki_p)
        vis = key_pos[None, :] <= tpos[:, None]
        sc = jnp.where(vis[None], sc, -jnp.inf)
        _, idx = lax.top_k(sc, topk)
        valid = idx <= tpos[None, :, None]
        return _attend_selected(qb, _gather_rows(k_p, idx), _gather_rows(v_p, idx), valid)

    att = lax.map(block, jnp.arange(lp // Q_BLOCK))
    return jnp.moveaxis(att, 0, 1).reshape(b, lp, ATT_W)[:, :l]


def setup_inputs(seed: int = 0) -> dict:
    key = jax.random.key(seed)
    ks = jax.random.split(key, 32)
    n_pages = PAST_LEN // PAGE_SIZE
    n_pool = (DEC_BATCH * n_pages * 5 + 3) // 4
    nrm = lambda k, shape, s: jax.random.normal(k, shape, jnp.float32) * s
    page_table = jax.random.permutation(ks[6], n_pool)[:DEC_BATCH * n_pages]
    page_table = page_table.reshape(DEC_BATCH, n_pages).astype(jnp.int32)
    return {
        "x_prompt": nrm(ks[0], (BATCH, SEQ, D_MODEL), 1.0),
        "x_sample": nrm(ks[1], (DEC_BATCH, DEC_SEQ, D_MODEL), 1.0),
        "cache_k": nrm(ks[2], (n_pool, PAGE_SIZE, N_HEADS, HEAD_DIM), 1.0),
        "cache_v": nrm(ks[3], (n_pool, PAGE_SIZE, N_HEADS, HEAD_DIM), 1.0),
        "cache_idx_k": nrm(ks[4], (n_pool, PAGE_SIZE, D_IDX), 1.0),
        "state_conv": nrm(ks[5], (DEC_BATCH, CONV_W - 1, D_CONV), 0.5),
        "page_table": page_table,
        "meta_tokens": nrm(ks[7], (N_META, D_MODEL), 1.0),
        "ffn1_w_in": nrm(ks[8], (D_MODEL, 2 * D_FF), D_MODEL ** -0.5),
        "ffn1_w_out": nrm(ks[9], (D_FF, D_MODEL), BETA * D_FF ** -0.5),
        "ln1_g": 1.0 + nrm(ks[10], (D_MODEL,), 0.02),
        "ln1_b": nrm(ks[11], (D_MODEL,), 0.02),
        "w_in": nrm(ks[12], (D_MODEL, N_IN_COLS), D_MODEL ** -0.5),
        "conv_w": nrm(ks[13], (CONV_W, D_CONV), CONV_W ** -0.5),
        "conv_b": nrm(ks[14], (D_CONV,), 0.02),
        "conv_ln_g": 1.0 + nrm(ks[15], (D_CONV,), 0.02),
        "conv_ln_b": nrm(ks[16], (D_CONV,), 0.02),
        "b_gate": nrm(ks[17], (2 * D_MODEL,), 0.02),
        "w_branch": nrm(ks[18], (2, BRANCH_W, D_MODEL), BETA * BRANCH_W ** -0.5),
        "w_out": nrm(ks[19], (D_MODEL, D_MODEL), BETA * D_MODEL ** -0.5),
        "ln2_g": 1.0 + nrm(ks[20], (D_MODEL,), 0.02),
        "ln2_b": nrm(ks[21], (D_MODEL,), 0.02),
        "ffn2_w_in": nrm(ks[22], (D_MODEL, 2 * D_FF), D_MODEL ** -0.5),
        "ffn2_w_out": nrm(ks[23], (D_FF, D_MODEL), BETA * D_FF ** -0.5),
        "ln3_g": 1.0 + nrm(ks[24], (D_MODEL,), 0.02),
        "ln3_b": nrm(ks[25], (D_MODEL,), 0.02),
    }


def reference(x_prompt, x_sample, cache_k, cache_v, cache_idx_k, state_conv, page_table,
              meta_tokens, ffn1_w_in, ffn1_w_out, ln1_g, ln1_b, w_in, conv_w, conv_b,
              conv_ln_g, conv_ln_b, b_gate, w_branch, w_out, ln2_g, ln2_b,
              ffn2_w_in, ffn2_w_out, ln3_g, ln3_b):
    b = x_prompt.shape[0]
    meta = jnp.broadcast_to(meta_tokens[None].astype(x_prompt.dtype), (b, N_META, D_MODEL))
    x = jnp.concatenate([meta, x_prompt], axis=1)
    l = x.shape[1]
    for _ in range(DEPTH):
        x1, z = _pre_mixer(x, ffn1_w_in, ffn1_w_out, ln1_g, ln1_b, w_in)
        u, q, k, v, qi, ki, wi, gates = _split_cols(z)
        u_ext = jnp.concatenate([jnp.zeros((b, CONV_W - 1, D_CONV), u.dtype), u], axis=1)
        u_conv = _conv_module(u_ext, conv_w, conv_b, conv_ln_g, conv_ln_b)
        att = _prompt_sparse_attention(q, k, v, qi, ki, wi)
        h = _merge(u_conv, att, gates, b_gate, w_branch, w_out)
        x = _post_mixer(x1, h, ln2_g, ln2_b, ffn2_w_in, ffn2_w_out, ln3_g, ln3_b)
        conv_prompt = u[:, l - (CONV_W - 1):]
        k_prompt, v_prompt, idxk_prompt = k, v, ki
    y_prompt = x[:, N_META:]

    bd, t = x_sample.shape[:2]
    ls = PAST_LEN + t
    xs = x_sample
    for _ in range(DEPTH):
        x1s, zs = _pre_mixer(xs, ffn1_w_in, ffn1_w_out, ln1_g, ln1_b, w_in)
        us, qs, k_s, v_s, qis, kis, wis, gates_s = _split_cols(zs)
        us_ext = jnp.concatenate([state_conv.astype(us.dtype), us], axis=1)
        uconv_s = _conv_module(us_ext, conv_w, conv_b, conv_ln_g, conv_ln_b)
        conv_sample = us_ext[:, t:]
        ki_past = cache_idx_k[page_table].reshape(bd, PAST_LEN, D_IDX)
        ki_all = jnp.concatenate([ki_past.astype(kis.dtype), kis], axis=1)
        sc = _index_scores(qis, wis, ki_all)
        qpos = PAST_LEN + jnp.arange(t)
        vis = jnp.arange(ls)[None, :] <= qpos[:, None]
        sc = jnp.where(vis[None], sc, -jnp.inf)
        _, idx = lax.top_k(sc, min(MAX_TOPK, ls // 4))
        valid = idx <= qpos[None, :, None]
        in_past = idx < PAST_LEN
        s_past = jnp.minimum(idx, PAST_LEN - 1)
        phys = jnp.take_along_axis(page_table, (s_past // PAGE_SIZE).reshape(bd, -1), axis=1)
        phys = phys.reshape(idx.shape)
        off = s_past % PAGE_SIZE
        s_new = jnp.clip(idx - PAST_LEN, 0, t - 1)
        sel = in_past[..., None, None]
        k_sel = jnp.where(sel, cache_k[phys, off].astype(k_s.dtype), _gather_rows(k_s, s_new))
        v_sel = jnp.where(sel, cache_v[phys, off].astype(v_s.dtype), _gather_rows(v_s, s_new))
        att_s = _attend_selected(qs, k_sel, v_sel, valid).reshape(bd, t, ATT_W)
        hs = _merge(uconv_s, att_s, gates_s, b_gate, w_branch, w_out)
        xs = _post_mixer(x1s, hs, ln2_g, ln2_b, ffn2_w_in, ffn2_w_out, ln3_g, ln3_b)
        k_sample, v_sample, idxk_sample = k_s, v_s, kis
    y_sample = xs

    return (y_prompt, y_sample, k_prompt, v_prompt, idxk_prompt, conv_prompt,
            k_sample, v_sample, idxk_sample, conv_sample)
```

```python
import functools

import jax
import jax.numpy as jnp
from jax import lax
from jax.experimental import pallas as pl
from jax.experimental.pallas import tpu as pltpu

D_MODEL = 1024
SEQ = 16384
N_META = 16
L_PROMPT = SEQ + N_META
DEC_BATCH = 128
PAST_LEN = 8192
PAGE_SIZE = 128
N_PAGES = PAST_LEN // PAGE_SIZE
D_CONV = 512
CONV_W = 31
N_HEADS = 8
HEAD_DIM = 64
ATT_W = N_HEADS * HEAD_DIM
D_IDX = 64
TOPK = 256
D_FF = 2816
ALPHA = 2.0 ** 0.25
LN_EPS = 1e-5

ROW_TILE = 128
LP = 16512
S_ROW0 = LP
S_ROW1 = S_ROW0 + DEC_BATCH
R_ALL = 16896

C_Q = 2 * D_CONV
C_K = C_Q + ATT_W
C_V = C_K + ATT_W
C_QI = C_V + ATT_W
C_KI = C_QI + N_HEADS * D_IDX
C_WI = C_KI + D_IDX
C_G = C_WI + N_HEADS

NEG_BIG = -1e30
VMEM_LIMIT = 60 * 1024 * 1024

BF16 = jnp.bfloat16
F32 = jnp.float32


def _nt_dot(a, b):
    return lax.dot_general(a, b, (((1,), (1,)), ((), ())), preferred_element_type=F32)


def _dot(a, b):
    return jnp.dot(a, b, preferred_element_type=F32)


def _sigmoid(x):
    return 1.0 / (1.0 + jnp.exp(-x))


def _layer_norm(x, g, b):
    mu = jnp.mean(x, axis=-1, keepdims=True)
    xc = x - mu
    var = jnp.mean(xc * xc, axis=-1, keepdims=True)
    return xc * lax.rsqrt(var + LN_EPS) * g + b


def _ffn_ln_kernel(x_ref, wg_ref, wu_ref, wo_ref, g_ref, b_ref, o_ref, acc_ref):
    j = pl.program_id(1)
    xb = x_ref[...].astype(BF16)
    gate = _dot(xb, wg_ref[...])
    up = _dot(xb, wu_ref[...])
    h = (gate * _sigmoid(gate)) * up
    part = _dot(h.astype(BF16), wo_ref[...])

    @pl.when(j == 0)
    def _():
        acc_ref[...] = part

    @pl.when(j > 0)
    def _():
        acc_ref[...] += part

    @pl.when(j == pl.num_programs(1) - 1)
    def _():
        y = ALPHA * x_ref[...] + 0.5 * acc_ref[...]
        o_ref[...] = _layer_norm(y, g_ref[...], b_ref[...])


def ffn_ln(x, w_in_bf, w_out_bf, g, b, *, tm, tf):
    rows, d = x.shape
    d_ff = w_out_bf.shape[0]
    nf = d_ff // tf
    return pl.pallas_call(
        _ffn_ln_kernel,
        grid=(rows // tm, nf),
        in_specs=[
            pl.BlockSpec((tm, d), lambda i, j: (i, 0)),
            pl.BlockSpec((d, tf), lambda i, j: (0, j)),
            pl.BlockSpec((d, tf), lambda i, j, nf=nf: (0, j + nf)),
            pl.BlockSpec((tf, d), lambda i, j: (j, 0)),
            pl.BlockSpec((1, d), lambda i, j: (0, 0)),
            pl.BlockSpec((1, d), lambda i, j: (0, 0)),
        ],
        out_specs=pl.BlockSpec((tm, d), lambda i, j: (i, 0)),
        out_shape=jax.ShapeDtypeStruct((rows, d), F32),
        scratch_shapes=[pltpu.VMEM((tm, d), F32)],
        compiler_params=pltpu.CompilerParams(
            dimension_semantics=("parallel", "arbitrary"), vmem_limit_bytes=VMEM_LIMIT),
        name="ffn_ln",
    )(x, w_in_bf, w_in_bf, w_out_bf, g.reshape(1, d), b.reshape(1, d))


def _in_proj_kernel(x_ref, wa_ref, wkw_ref, wk2_ref, wg_ref, bg_ref,
                    u_ref, qb_ref, k_ref, v_ref, kb_ref, vb_ref, qib_ref, kiwi_ref, ki2_ref, gs_ref):
    xb = x_ref[...].astype(BF16)
    glu = _dot(xb, wa_ref[:, 0:C_Q])
    u_ref[...] = glu[:, :D_CONV] * _sigmoid(glu[:, D_CONV:])
    q = _dot(xb, wa_ref[:, C_Q:C_K])
    qb_ref[...] = (q * (HEAD_DIM ** -0.5)).astype(BF16)
    k = _dot(xb, wa_ref[:, C_K:C_V])
    k_ref[...] = k
    kb_ref[...] = k.astype(BF16)
    v = _dot(xb, wa_ref[:, C_V:C_QI])
    v_ref[...] = v
    vb_ref[...] = v.astype(BF16)
    qib_ref[...] = _dot(xb, wa_ref[:, C_QI:C_KI]).astype(BF16)
    kiwi_ref[...] = _dot(xb, wkw_ref[...])
    ki2_ref[...] = _dot(xb, wk2_ref[...]).astype(BF16)
    gs_ref[...] = _sigmoid(_dot(xb, wg_ref[...]) + bg_ref[...])


def in_proj(x1, wa, wkw, wk2, wg, b_gate, *, tm):
    rows, d = x1.shape
    row_spec = lambda w: pl.BlockSpec((tm, w), lambda i: (i, 0))
    full = lambda a: pl.BlockSpec(a.shape, lambda i: (0, 0))
    bg = b_gate.reshape(1, -1)
    outs = [
        (D_CONV, F32),
        (ATT_W, BF16),
        (ATT_W, F32),
        (ATT_W, F32),
        (ATT_W, BF16),
        (ATT_W, BF16),
        (ATT_W, BF16),
        (128, F32),
        (128, BF16),
        (2 * D_MODEL, F32),
    ]
    return pl.pallas_call(
        _in_proj_kernel,
        grid=(rows // tm,),
        in_specs=[row_spec(d), full(wa), full(wkw), full(wk2), full(wg), full(bg)],
        out_specs=[row_spec(w) for w, _ in outs],
        out_shape=[jax.ShapeDtypeStruct((rows, w), dt) for w, dt in outs],
        compiler_params=pltpu.CompilerParams(
            dimension_semantics=("parallel",), vmem_limit_bytes=VMEM_LIMIT),
        name="in_proj",
    )(x1, wa, wkw, wk2, wg, bg)


CONV_HALO = 32
CONV_SUB = 64


def _conv_post(c, cb_ref, g_ref, b_ref):
    y = _layer_norm(c + cb_ref[...], g_ref[...], b_ref[...])
    return y * _sigmoid(y)


def _conv_prompt_kernel(halo_ref, cur_ref, w_ref, cb_ref, g_ref, b_ref, o_ref, ext_ref, *, tm):
    i = pl.program_id(0)

    @pl.when(i == 0)
    def _():
        ext_ref[0:CONV_HALO, :] = jnp.zeros((CONV_HALO, D_CONV), F32)

    @pl.when(i > 0)
    def _():
        ext_ref[0:CONV_HALO, :] = halo_ref[...]

    ext_ref[CONV_HALO:, :] = cur_ref[...]
    off = CONV_HALO - (CONV_W - 1)
    for r0 in range(0, tm, CONV_SUB):
        acc = jnp.zeros((CONV_SUB, D_CONV), F32)
        for j in range(CONV_W):
            acc = acc + ext_ref[r0 + off + j:r0 + off + j + CONV_SUB, :] * w_ref[j:j + 1, :]
        o_ref[r0:r0 + CONV_SUB, :] = _conv_post(acc, cb_ref, g_ref, b_ref)


def conv_prompt(u, conv_w, conv_b, g, b, *, rows, tm):
    per = tm // CONV_HALO
    vec = lambda a: a.reshape(1, D_CONV)
    full = lambda shape: pl.BlockSpec(shape, lambda i: (0, 0))
    return pl.pallas_call(
        functools.partial(_conv_prompt_kernel, tm=tm),
        grid=(rows // tm,),
        in_specs=[
            pl.BlockSpec((CONV_HALO, D_CONV), lambda i: (jnp.maximum(i * per - 1, 0), 0)),
            pl.BlockSpec((tm, D_CONV), lambda i: (i, 0)),
            full((CONV_W, D_CONV)), full((1, D_CONV)), full((1, D_CONV)), full((1, D_CONV)),
        ],
        out_specs=pl.BlockSpec((tm, D_CONV), lambda i: (i, 0)),
        out_shape=jax.ShapeDtypeStruct((rows, D_CONV), F32),
        scratch_shapes=[pltpu.VMEM((CONV_HALO + tm, D_CONV), F32)],
        compiler_params=pltpu.CompilerParams(dimension_semantics=("arbitrary",)),
        name="conv_prompt",
    )(u, u, conv_w, vec(conv_b), vec(g), vec(b))


def _conv_sample_kernel(state_ref, us_ref, w_ref, cb_ref, g_ref, b_ref, o_ref):
    acc = us_ref[...] * w_ref[CONV_W - 1:CONV_W, :]
    for j in range(CONV_W - 1):
        acc = acc + state_ref[j] * w_ref[j:j + 1, :]
    o_ref[...] = _conv_post(acc, cb_ref, g_ref, b_ref)


def conv_sample(state_t, u, conv_w, conv_b, g, b, *, row0):
    nb = state_t.shape[1]
    vec = lambda a: a.reshape(1, D_CONV)
    full = lambda shape: pl.BlockSpec(shape, lambda i: (0,) * len(shape))
    assert row0 % nb == 0
    return pl.pallas_call(
        _conv_sample_kernel,
        grid=(1,),
        in_specs=[
            full(state_t.shape),
            pl.BlockSpec((nb, D_CONV), lambda i: (row0 // nb, 0)),
            full((CONV_W, D_CONV)), full((1, D_CONV)), full((1, D_CONV)), full((1, D_CONV)),
        ],
        out_specs=full((nb, D_CONV)),
        out_shape=jax.ShapeDtypeStruct((nb, D_CONV), F32),
        name="conv_sample",
    )(state_t, u, conv_w, vec(conv_b), vec(g), vec(b))


KC = 4 * ROW_TILE
MAX_BISECTIONS = 400
NEG_INF = float("-inf")
POS_INF = float("inf")


def _lane_tiles(x):
    return [x[:, t * ROW_TILE:(t + 1) * ROW_TILE] for t in range(x.shape[1] // ROW_TILE)]


def _fold(x, op=jnp.add):
    return functools.reduce(op, _lane_tiles(x))


def _count_rows(sc_ref, nch, pred):
    def body(c, acc):
        return acc + _fold(jnp.where(pred(sc_ref[c], c), 1.0, 0.0))

    acc = lax.fori_loop(0, nch, body, jnp.zeros((ROW_TILE, ROW_TILE), F32))
    return jnp.sum(acc, axis=1, keepdims=True)


def _last_kept_position(tied_before, need, n_pos):
    nbits = (n_pos - 1).bit_length()

    def bit_body(b, j):
        cand = j | lax.shift_left(jnp.int32(1), nbits - 1 - b)
        return jnp.where(tied_before(cand) < need, cand, j)

    return lax.fori_loop(0, nbits, bit_body, jnp.zeros((ROW_TILE, 1), jnp.int32))


def _select_topk(sc_ref, nch, n_vis, topk):
    kf = float(topk)
    wide = lambda v: jnp.broadcast_to(v, (ROW_TILE, KC))

    def range_body(c, carry):
        lo, hi = carry
        sc = sc_ref[c]
        seen = jnp.where(sc == NEG_INF, POS_INF, sc)
        lo = jnp.minimum(lo, _fold(seen, jnp.minimum))
        hi = jnp.maximum(hi, _fold(sc, jnp.maximum))
        return lo, hi

    lo_t, hi_t = lax.fori_loop(
        0, nch, range_body,
        (jnp.full((ROW_TILE, ROW_TILE), POS_INF, F32), jnp.full((ROW_TILE, ROW_TILE), NEG_INF, F32)))
    rmin = jnp.min(lo_t, axis=1, keepdims=True)
    rmax = jnp.max(hi_t, axis=1, keepdims=True)
    needs = n_vis > kf

    lo0 = rmin
    hi0 = rmax + (0.5 * (jnp.abs(rmax) + jnp.abs(rmin)) + 1e-30)

    def searching(lo, hi, cnt_lo):
        mid = lo + 0.5 * (hi - lo)
        return mid, needs & (cnt_lo != kf) & (mid > lo) & (mid < hi)

    def any_row(flag):
        return jnp.max(jnp.where(flag, 1.0, 0.0))

    def cond(st):
        return (st[0] < MAX_BISECTIONS) & (st[1] > 0.0)

    def body(st):
        it, _, lo, hi, cnt_lo, cnt_hi = st
        mid, open_ = searching(lo, hi, cnt_lo)
        mid_w = wide(mid)
        cnt = _count_rows(sc_ref, nch, lambda sc, c: sc >= mid_w)
        up = open_ & (cnt >= kf)
        dn = open_ & (cnt < kf)
        lo = jnp.where(up, mid, lo)
        cnt_lo = jnp.where(up, cnt, cnt_lo)
        hi = jnp.where(dn, mid, hi)
        cnt_hi = jnp.where(dn, cnt, cnt_hi)
        _, still = searching(lo, hi, cnt_lo)
        return it + 1, any_row(still), lo, hi, cnt_lo, cnt_hi

    _, open0 = searching(lo0, hi0, n_vis)
    _, _, lo, hi, cnt_lo, cnt_hi = lax.while_loop(
        cond, body, (jnp.int32(0), any_row(open0), lo0, hi0, n_vis, jnp.zeros_like(n_vis)))
    thr = jnp.where(needs, lo, rmin)

    surplus = needs & (cnt_lo > kf)

    @pl.when(any_row(surplus) > 0.0)
    def _():
        lo_w, hi_w, surplus_w = wide(lo), wide(hi), wide(surplus)
        lane = lax.broadcasted_iota(jnp.int32, (ROW_TILE, KC), 1)
        need = kf - cnt_hi

        def tied_before(bound):
            bound_w = wide(bound)
            return _count_rows(
                sc_ref, nch, lambda sc, c: (sc >= lo_w) & (sc < hi_w) & (lane + c * KC < bound_w))

        last_w = wide(_last_kept_position(tied_before, need, sc_ref.shape[0] * KC))

        def demote(c, carry):
            sc = sc_ref[c]
            drop = surplus_w & (sc >= lo_w) & (sc < hi_w) & (lane + c * KC > last_w)
            sc_ref[c] = jnp.where(drop, NEG_INF, sc)
            return carry

        lax.fori_loop(0, nch, demote, 0)

    return thr


def _prompt_attn_kernel(qi_ref, kiwi_ref, q_ref, ki2_ref, k_ref, v_ref, o_ref,
                        sc_ref, m_ref, l_ref, acc_ref, p_ref, *, topk):
    i = pl.program_id(0)
    nch = (i * ROW_TILE) // KC + 1
    lane = lax.broadcasted_iota(jnp.int32, (ROW_TILE, ROW_TILE), 1)
    lo_half = lane < HEAD_DIM
    qpos = i * ROW_TILE + lax.broadcasted_iota(jnp.int32, (ROW_TILE, KC), 0)
    kiota = lax.broadcasted_iota(jnp.int32, (ROW_TILE, KC), 1)

    def head_lanes(ref, h):
        p = h // 2
        blk = ref[:, p * ROW_TILE:(p + 1) * ROW_TILE]
        keep = lo_half if h % 2 == 0 else jnp.logical_not(lo_half)
        return jnp.where(keep, blk, jnp.zeros_like(blk))

    qis = [head_lanes(qi_ref, h) for h in range(N_HEADS)]
    wi = kiwi_ref[...]
    wis = [jnp.broadcast_to(wi[:, D_IDX + h:D_IDX + h + 1], (ROW_TILE, ROW_TILE)) for h in range(N_HEADS)]

    def score_body(c, carry):
        k0 = pl.multiple_of(c * KC, KC)
        kic = ki2_ref[pl.ds(k0, KC), :]
        parts = [jnp.zeros((ROW_TILE, ROW_TILE), F32)] * (KC // ROW_TILE)
        for h in range(N_HEADS):
            s = jnp.maximum(_nt_dot(qis[h], kic), 0.0)
            parts = [acc + tile * wis[h] for acc, tile in zip(parts, _lane_tiles(s))]
        sc = jnp.concatenate(parts, axis=1)
        sc_ref[c] = jnp.where(kiota + k0 <= qpos, sc, NEG_INF)
        return carry

    lax.fori_loop(0, nch, score_body, 0)

    n_vis = (qpos[:, 0:1] + 1).astype(F32)
    thr_w = jnp.broadcast_to(_select_topk(sc_ref, nch, n_vis, topk), (ROW_TILE, KC))

    qs = [head_lanes(q_ref, h) for h in range(N_HEADS)]
    m_ref[...] = jnp.full(m_ref.shape, NEG_BIG, F32)

    def masked_scores(c):
        k0 = pl.multiple_of(c * KC, KC)
        bias = jnp.where(sc_ref[c] >= thr_w, 0.0, NEG_BIG)
        return k0, bias

    def max_body(c, carry):
        k0, bias = masked_scores(c)
        for p in range(N_HEADS // 2):
            kp = k_ref[pl.ds(k0, KC), p * ROW_TILE:(p + 1) * ROW_TILE]
            for h in (2 * p, 2 * p + 1):
                s = _nt_dot(qs[h], kp) + bias
                m_ref[h] = jnp.maximum(m_ref[h], _fold(s, jnp.maximum))
        return carry

    lax.fori_loop(0, nch, max_body, 0)
    for h in range(N_HEADS):
        m_ref[h] = jnp.broadcast_to(jnp.max(m_ref[h], axis=1, keepdims=True), (ROW_TILE, ROW_TILE))

    l_ref[...] = jnp.zeros(l_ref.shape, F32)
    acc_ref[...] = jnp.zeros(acc_ref.shape, F32)

    p_ref[...] = jnp.zeros(p_ref.shape, BF16)

    def weighted_values(c, slot):
        k0 = pl.multiple_of(c * KC, KC)
        for p in range(N_HEADS // 2):
            vp = v_ref[pl.ds(k0, KC), p * ROW_TILE:(p + 1) * ROW_TILE]
            for h in (2 * p, 2 * p + 1):
                acc_ref[h] += _dot(p_ref[slot, h], vp)

    def pv_body(c, carry):
        slot = c % 2
        k0, bias = masked_scores(c)
        for p in range(N_HEADS // 2):
            kp = k_ref[pl.ds(k0, KC), p * ROW_TILE:(p + 1) * ROW_TILE]
            for h in (2 * p, 2 * p + 1):
                s = _nt_dot(qs[h], kp) + bias
                m = m_ref[h]
                es = [jnp.exp(tile - m) for tile in _lane_tiles(s)]
                l_ref[h] += functools.reduce(jnp.add, es)
                p_ref[slot, h] = jnp.concatenate(es, axis=1).astype(BF16)
        weighted_values(jnp.maximum(c - 1, 0), 1 - slot)
        return carry

    lax.fori_loop(0, nch, pv_body, 0)
    weighted_values(nch - 1, (nch - 1) % 2)

    for p in range(N_HEADS // 2):
        even = acc_ref[2 * p] / jnp.sum(l_ref[2 * p], axis=1, keepdims=True)
        odd = acc_ref[2 * p + 1] / jnp.sum(l_ref[2 * p + 1], axis=1, keepdims=True)
        o_ref[:, p * ROW_TILE:(p + 1) * ROW_TILE] = jnp.where(lo_half, even, odd)


def prompt_attention(qi_bf, kiwi, q_bf, ki2_bf, k_bf, v_bf, *, n_blocks, topk):
    rows = k_bf.shape[0]
    blk = lambda w: pl.BlockSpec((ROW_TILE, w), lambda i: (i, 0))
    full = lambda a: pl.BlockSpec(a.shape, lambda i: (0, 0))
    head_tiles = pltpu.VMEM((N_HEADS, ROW_TILE, ROW_TILE), F32)
    return pl.pallas_call(
        functools.partial(_prompt_attn_kernel, topk=topk),
        grid=(n_blocks,),
        in_specs=[blk(ATT_W), blk(128), blk(ATT_W), full(ki2_bf), full(k_bf), full(v_bf)],
        out_specs=blk(ATT_W),
        out_shape=jax.ShapeDtypeStruct((n_blocks * ROW_TILE, ATT_W), F32),
        scratch_shapes=[pltpu.VMEM((rows // KC, ROW_TILE, KC), F32), head_tiles, head_tiles, head_tiles,
                        pltpu.VMEM((2, N_HEADS, ROW_TILE, KC), BF16)],
        compiler_params=pltpu.CompilerParams(
            dimension_semantics=("arbitrary",), vmem_limit_bytes=VMEM_LIMIT),
        name="prompt_attn",
    )(qi_bf, kiwi, q_bf, ki2_bf, k_bf, v_bf)


HEAD_ROWS = 16
PAGE_GROUP = 8


def _sample_index_kernel(pt_ref, qi_ref, wi_ref, kin_ref, cik_ref, sc_out, thr_out,
                         kibuf, sc_ref, sem, *, topk, n_pages):
    b = pl.program_id(0)

    def page_copy(pg):
        return pltpu.make_async_copy(cik_ref.at[pt_ref[b * n_pages + pg]], kibuf.at[pg], sem.at[0])

    def start(pg, carry):
        page_copy(pg).start()
        return carry

    def wait(pg, carry):
        page_copy(pg).wait()
        return carry

    lax.fori_loop(0, n_pages, start, 0)
    lax.fori_loop(0, n_pages, wait, 0)

    qi = qi_ref[0]
    wi = wi_ref[0]
    pages_per_chunk = KC // PAGE_SIZE
    for c in range(n_pages // pages_per_chunk):
        parts = []
        for t in range(pages_per_chunk):
            kit = kibuf[c * pages_per_chunk + t].astype(BF16)
            s = jnp.maximum(_dot(qi, kit), 0.0) * wi
            parts.append(jnp.sum(s, axis=0, keepdims=True))
        sc_ref[c, pl.ds(b, 1), :] = jnp.concatenate(parts, axis=1)

    kin = kin_ref[0].astype(BF16).astype(F32)
    s_new = jnp.sum(qi.astype(F32) * kin, axis=1, keepdims=True)
    sc_new = jnp.sum(jnp.maximum(s_new, 0.0) * wi, axis=0, keepdims=True)
    lane = lax.broadcasted_iota(jnp.int32, (1, KC), 1)
    new_chunk = n_pages // pages_per_chunk
    sc_ref[new_chunk, pl.ds(b, 1), :] = jnp.where(lane == 0, sc_new, NEG_INF)

    @pl.when(b == pl.num_programs(0) - 1)
    def _():
        nch = new_chunk + 1
        n_vis = jnp.full((ROW_TILE, 1), float(n_pages * PAGE_SIZE + 1), F32)
        thr = _select_topk(sc_ref, nch, n_vis, topk)
        thr_out[...] = jnp.broadcast_to(thr, (ROW_TILE, ROW_TILE))
        for c in range(nch):
            sc_out[:, c * KC:(c + 1) * KC] = sc_ref[c]


def sample_index(page_table, qi_s, wi_s, ki_new, cache_ik_t, *, topk):
    nb, n_pages = page_table.shape
    assert nb == ROW_TILE and cache_ik_t.shape[1:] == (D_IDX, PAGE_SIZE)
    nch = n_pages * PAGE_SIZE // KC + 1
    per_q = lambda shape: pl.BlockSpec((1,) + shape, lambda b, pt: (b, 0, 0))
    full = lambda shape: pl.BlockSpec(shape, lambda b, pt: (0, 0))
    grid_spec = pltpu.PrefetchScalarGridSpec(
        num_scalar_prefetch=1,
        grid=(nb,),
        in_specs=[per_q((HEAD_ROWS, D_IDX)), per_q((HEAD_ROWS, 1)), per_q((1, D_IDX)),
                  pl.BlockSpec(memory_space=pl.ANY)],
        out_specs=[full((nb, nch * KC)), full((nb, ROW_TILE))],
        scratch_shapes=[
            pltpu.VMEM((n_pages, D_IDX, PAGE_SIZE), F32),
            pltpu.VMEM((nch, ROW_TILE, KC), F32),
            pltpu.SemaphoreType.DMA((1,)),
        ],
    )
    return pl.pallas_call(
        functools.partial(_sample_index_kernel, topk=topk, n_pages=n_pages),
        grid_spec=grid_spec,
        out_shape=[jax.ShapeDtypeStruct((nb, nch * KC), F32), jax.ShapeDtypeStruct((nb, ROW_TILE), F32)],
        compiler_params=pltpu.CompilerParams(
            dimension_semantics=("arbitrary",), vmem_limit_bytes=VMEM_LIMIT),
        name="sample_index",
    )(page_table.reshape(-1), qi_s, wi_s, ki_new, cache_ik_t)


def _sample_attn_kernel(pt_ref, bdq_ref, sc_ref, thr_ref, kn_ref, vn_ref, ck_ref, cv_ref, o_ref,
                        buf, s_ref, sem, *, n_pages):
    b = pl.program_id(0)
    past = n_pages * PAGE_SIZE
    units_per_cache = n_pages // PAGE_GROUP
    n_units = 2 * units_per_cache

    def unit_copies(q, t):
        cache = ck_ref if t < units_per_cache else cv_ref
        first = (t % units_per_cache) * PAGE_GROUP
        slot = t % 2
        return [pltpu.make_async_copy(cache.at[pt_ref[q * n_pages + first + j]], buf.at[slot, j], sem.at[slot])
                for j in range(PAGE_GROUP)]

    def start_unit(q, t):
        for cp in unit_copies(q, t):
            cp.start()

    @pl.when(b == 0)
    def _():
        start_unit(b, 0)

    bdq = bdq_ref[0]
    out = jnp.zeros((HEAD_ROWS, ATT_W), F32)
    denom = None
    for t in range(n_units):
        for cp in unit_copies(b, t):
            cp.wait()
        if t + 1 < n_units:
            start_unit(b, t + 1)
        else:
            @pl.when(b + 1 < pl.num_programs(0))
            def _():
                start_unit(b + 1, 0)

        slot = t % 2
        for j in range(PAGE_GROUP):
            pg = (t % units_per_cache) * PAGE_GROUP + j
            lanes = slice(pg * PAGE_SIZE, (pg + 1) * PAGE_SIZE)
            page = buf[slot, j].reshape(ATT_W, PAGE_SIZE).astype(BF16)
            if t < units_per_cache:
                s_ref[:, lanes] = _dot(bdq, page)
            else:
                out = out + _nt_dot(s_ref[:, lanes].astype(BF16), page)

        if t == units_per_cache - 1:
            sc = sc_ref[0]
            thr = thr_ref[0][:, 0:1]
            sel = sc[:, :past] >= thr
            sel_new = sc[:, past:past + 1] >= thr
            s = s_ref[...] + jnp.where(sel, 0.0, NEG_BIG)
            kn = kn_ref[0].astype(BF16).astype(F32)
            s_new = jnp.sum(bdq.astype(F32) * kn, axis=1, keepdims=True)
            s_new = jnp.where(sel_new, s_new, NEG_BIG)
            m = jnp.maximum(jnp.max(s, axis=1, keepdims=True), s_new)
            pr = jnp.exp(s - m)
            p_new = jnp.where(sel_new, jnp.exp(s_new - m), 0.0)
            denom = jnp.sum(pr, axis=1, keepdims=True) + p_new
            s_ref[...] = pr
            vn = vn_ref[0].astype(BF16).astype(F32)
            out = p_new.astype(BF16).astype(F32) * vn

    out = out / denom
    head_of_lane = lax.broadcasted_iota(jnp.int32, (HEAD_ROWS, ATT_W), 1) // HEAD_DIM
    head_of_row = lax.broadcasted_iota(jnp.int32, (HEAD_ROWS, ATT_W), 0)
    o_ref[0] = jnp.sum(jnp.where(head_of_lane == head_of_row, out, 0.0), axis=0, keepdims=True)


def sample_attention(page_table, bdq, sc_all, thr, k_new, v_new, cache_k_t, cache_v_t):
    nb, n_pages = page_table.shape
    assert cache_k_t.shape[1:] == (N_HEADS, HEAD_DIM, PAGE_SIZE) and n_pages % (2 * PAGE_GROUP) == 0
    past = n_pages * PAGE_SIZE
    per_q = lambda shape: pl.BlockSpec((1,) + shape, lambda b, pt: (b, 0, 0))
    any_spec = pl.BlockSpec(memory_space=pl.ANY)
    grid_spec = pltpu.PrefetchScalarGridSpec(
        num_scalar_prefetch=1,
        grid=(nb,),
        in_specs=[per_q((HEAD_ROWS, ATT_W)), per_q((1, sc_all.shape[-1])), per_q((1, ROW_TILE)),
                  per_q((1, ATT_W)), per_q((1, ATT_W)), any_spec, any_spec],
        out_specs=per_q((1, ATT_W)),
        scratch_shapes=[
            pltpu.VMEM((2, PAGE_GROUP, N_HEADS, HEAD_DIM, PAGE_SIZE), F32),
            pltpu.VMEM((HEAD_ROWS, past), F32),
            pltpu.SemaphoreType.DMA((2,)),
        ],
    )
    return pl.pallas_call(
        functools.partial(_sample_attn_kernel, n_pages=n_pages),
        grid_spec=grid_spec,
        out_shape=jax.ShapeDtypeStruct((nb, 1, ATT_W), F32),
        compiler_params=pltpu.CompilerParams(
            dimension_semantics=("arbitrary",), vmem_limit_bytes=VMEM_LIMIT),
        name="sample_attn",
    )(page_table.reshape(-1), bdq, sc_all.reshape(nb, 1, -1), thr.reshape(nb, 1, ROW_TILE),
      k_new, v_new, cache_k_t, cache_v_t)


def _merge_ln_kernel(uc_ref, att_ref, gs_ref, x1_ref, wb0_ref, wb1_ref, wo_ref, g_ref, b_ref, o_ref):
    conv_br = _dot(uc_ref[...].astype(BF16), wb0_ref[...])
    att_br = _dot(att_ref[...].astype(BF16), wb1_ref[...])
    m = gs_ref[:, :D_MODEL] * conv_br + gs_ref[:, D_MODEL:] * att_br
    h = _dot(m.astype(BF16), wo_ref[...])
    o_ref[...] = _layer_norm(ALPHA * x1_ref[...] + h, g_ref[...], b_ref[...])


def merge_ln(u_conv, att, gs, x1, wb0, wb1, wo, g, b, *, tm):
    rows, d = x1.shape
    row_spec = lambda w: pl.BlockSpec((tm, w), lambda i: (i, 0))
    full = lambda a: pl.BlockSpec(a.shape, lambda i: (0, 0))
    g2, b2 = g.reshape(1, d), b.reshape(1, d)
    return pl.pallas_call(
        _merge_ln_kernel,
        grid=(rows // tm,),
        in_specs=[row_spec(D_CONV), row_spec(ATT_W), row_spec(2 * d), row_spec(d),
                  full(wb0), full(wb1), full(wo), full(g2), full(b2)],
        out_specs=row_spec(d),
        out_shape=jax.ShapeDtypeStruct((rows, d), F32),
        compiler_params=pltpu.CompilerParams(
            dimension_semantics=("parallel",), vmem_limit_bytes=VMEM_LIMIT),
        name="merge_ln",
    )(u_conv, att, gs, x1, wb0, wb1, wo, g2, b2)


def kernel(x_prompt, x_sample, cache_k, cache_v, cache_idx_k, state_conv, page_table, meta_tokens,
           ffn1_w_in, ffn1_w_out, ln1_g, ln1_b, w_in, conv_w, conv_b, conv_ln_g, conv_ln_b, b_gate,
           w_branch, w_out, ln2_g, ln2_b, ffn2_w_in, ffn2_w_out, ln3_g, ln3_b):
    topk_prompt = min(TOPK, SEQ // 4)
    topk_sample = min(TOPK, (PAST_LEN + 1) // 4)

    x_all = jnp.concatenate([
        meta_tokens.astype(F32), x_prompt[0],
        jnp.zeros((LP - L_PROMPT, D_MODEL), F32), x_sample[:, 0],
        jnp.zeros((R_ALL - S_ROW1, D_MODEL), F32)], axis=0)
    wa = w_in[:, :C_KI].astype(BF16)
    w_ki = w_in[:, C_KI:C_WI]
    wkw = jnp.concatenate([w_in[:, C_KI:C_G], jnp.zeros((D_MODEL, 128 - D_IDX - N_HEADS), F32)],
                          axis=1).astype(BF16)
    wk2 = jnp.concatenate([w_ki, w_ki], axis=1).astype(BF16)
    wg = w_in[:, C_G:].astype(BF16)

    x1 = ffn_ln(x_all, ffn1_w_in.astype(BF16), ffn1_w_out.astype(BF16), ln1_g, ln1_b, tm=512, tf=D_FF // 2)
    u, q_bf, k, v, k_bf, v_bf, qi_bf, kiwi, ki2_bf, gs = in_proj(x1, wa, wkw, wk2, wg, b_gate, tm=384)
    sample_rows = lambda a: a[S_ROW0:S_ROW1]
    tail_rows = jnp.zeros((R_ALL - S_ROW1, ATT_W), F32)

    uc_prompt = conv_prompt(u, conv_w, conv_b, conv_ln_g, conv_ln_b, rows=LP, tm=384)
    uc_sample = conv_sample(jnp.transpose(state_conv, (1, 0, 2)), u, conv_w, conv_b, conv_ln_g, conv_ln_b,
                            row0=S_ROW0)
    u_conv = jnp.concatenate([uc_prompt, uc_sample, tail_rows], axis=0)

    att_prompt = prompt_attention(qi_bf, kiwi, q_bf, ki2_bf, k_bf, v_bf,
                                  n_blocks=LP // ROW_TILE, topk=topk_prompt)
    pad_heads = lambda a: jnp.pad(a, ((0, 0), (0, HEAD_ROWS - N_HEADS), (0, 0)))
    kiwi_s, k_s, v_s, u_s = sample_rows(kiwi), sample_rows(k), sample_rows(v), sample_rows(u)
    qi_s = pad_heads(sample_rows(qi_bf).reshape(DEC_BATCH, N_HEADS, D_IDX))
    wi_s = pad_heads(kiwi_s[:, D_IDX:D_IDX + N_HEADS].reshape(DEC_BATCH, N_HEADS, 1))
    head_of_lane = jnp.arange(ATT_W) // HEAD_DIM
    bdq = jnp.where(head_of_lane[None, None, :] == jnp.arange(HEAD_ROWS)[None, :, None],
                    sample_rows(q_bf)[:, None, :], jnp.zeros((), BF16))
    cache_ik_t = jnp.transpose(cache_idx_k, (0, 2, 1))
    cache_k_t = jnp.transpose(cache_k, (0, 2, 3, 1))
    cache_v_t = jnp.transpose(cache_v, (0, 2, 3, 1))
    sc_all, thr = sample_index(page_table, qi_s, wi_s, kiwi_s[:, :D_IDX].reshape(DEC_BATCH, 1, D_IDX),
                               cache_ik_t, topk=topk_sample)
    att_sample = sample_attention(
        page_table, bdq, sc_all, thr,
        k_s.reshape(DEC_BATCH, 1, ATT_W), v_s.reshape(DEC_BATCH, 1, ATT_W), cache_k_t, cache_v_t)
    att = jnp.concatenate([att_prompt, att_sample[:, 0], tail_rows], axis=0)

    x2 = merge_ln(u_conv, att, gs, x1, w_branch[0].astype(BF16), w_branch[1].astype(BF16),
                  w_out.astype(BF16), ln2_g, ln2_b, tm=512)
    y = ffn_ln(x2, ffn2_w_in.astype(BF16), ffn2_w_out.astype(BF16), ln3_g, ln3_b, tm=512, tf=D_FF // 2)

    hd = (N_HEADS, HEAD_DIM)
    y_prompt = y[N_META:L_PROMPT][None]
    y_sample = sample_rows(y)[:, None]
    k_prompt = k[:L_PROMPT].reshape((1, L_PROMPT) + hd)
    v_prompt = v[:L_PROMPT].reshape((1, L_PROMPT) + hd)
    idxk_prompt = kiwi[:L_PROMPT, :D_IDX][None]
    conv_prompt_state = u[L_PROMPT - (CONV_W - 1):L_PROMPT][None]
    k_sample = k_s.reshape((DEC_BATCH, 1) + hd)
    v_sample = v_s.reshape((DEC_BATCH, 1) + hd)
    idxk_sample = kiwi_s[:, :D_IDX][:, None]
    conv_sample_state = jnp.concatenate([state_conv[:, 1:].astype(F32), u_s[:, None]], axis=1)
    return (y_prompt, y_sample, k_prompt, v_prompt, idxk_prompt, conv_prompt_state,
            k_sample, v_sample, idxk_sample, conv_sample_state)
```

```python
import functools

import jax
import jax.numpy as jnp
from jax import lax
from jax.experimental import pallas as pl
from jax.experimental.pallas import tpu as pltpu

D_MODEL = 1024
SEQ = 16384
N_META = 16
L_PROMPT = SEQ + N_META
DEC_BATCH = 128
PAST_LEN = 8192
PAGE_SIZE = 128
N_PAGES = PAST_LEN // PAGE_SIZE
D_CONV = 512
CONV_W = 31
N_HEADS = 8
HEAD_DIM = 64
ATT_W = N_HEADS * HEAD_DIM
D_IDX = 64
TOPK = 256
D_FF = 2816
ALPHA = 2.0 ** 0.25
LN_EPS = 1e-5

ROW_TILE = 128
LP = 16512
S_ROW0 = LP
S_ROW1 = S_ROW0 + DEC_BATCH
R_ALL = 16896

C_Q = 2 * D_CONV
C_K = C_Q + ATT_W
C_V = C_K + ATT_W
C_QI = C_V + ATT_W
C_KI = C_QI + N_HEADS * D_IDX
C_WI = C_KI + D_IDX
C_G = C_WI + N_HEADS

NEG_BIG = -1e30
VMEM_LIMIT = 60 * 1024 * 1024

BF16 = jnp.bfloat16
F32 = jnp.float32


def _nt_dot(a, b):
    return lax.dot_general(a, b, (((1,), (1,)), ((), ())), preferred_element_type=F32)


def _dot(a, b):
    return jnp.dot(a, b, preferred_element_type=F32)


def _sigmoid(x):
    return 1.0 / (1.0 + jnp.exp(-x))


def _layer_norm(x, g, b):
    mu = jnp.mean(x, axis=-1, keepdims=True)
    xc = x - mu
    var = jnp.mean(xc * xc, axis=-1, keepdims=True)
    return xc * lax.rsqrt(var + LN_EPS) * g + b


def _ffn_ln_kernel(x_ref, wg_ref, wu_ref, wo_ref, g_ref, b_ref, o_ref, acc_ref):
    j = pl.program_id(1)
    xb = x_ref[...].astype(BF16)
    gate = _dot(xb, wg_ref[...])
    up = _dot(xb, wu_ref[...])
    h = (gate * _sigmoid(gate)) * up
    part = _dot(h.astype(BF16), wo_ref[...])

    @pl.when(j == 0)
    def _():
        acc_ref[...] = part

    @pl.when(j > 0)
    def _():
        acc_ref[...] += part

    @pl.when(j == pl.num_programs(1) - 1)
    def _():
        y = ALPHA * x_ref[...] + 0.5 * acc_ref[...]
        o_ref[...] = _layer_norm(y, g_ref[...], b_ref[...])


def ffn_ln(x, w_in_bf, w_out_bf, g, b, *, tm, tf):
    rows, d = x.shape
    d_ff = w_out_bf.shape[0]
    nf = d_ff // tf
    return pl.pallas_call(
        _ffn_ln_kernel,
        grid=(rows // tm, nf),
        in_specs=[
            pl.BlockSpec((tm, d), lambda i, j: (i, 0)),
            pl.BlockSpec((d, tf), lambda i, j: (0, j)),
            pl.BlockSpec((d, tf), lambda i, j, nf=nf: (0, j + nf)),
            pl.BlockSpec((tf, d), lambda i, j: (j, 0)),
            pl.BlockSpec((1, d), lambda i, j: (0, 0)),
            pl.BlockSpec((1, d), lambda i, j: (0, 0)),
        ],
        out_specs=pl.BlockSpec((tm, d), lambda i, j: (i, 0)),
        out_shape=jax.ShapeDtypeStruct((rows, d), F32),
        scratch_shapes=[pltpu.VMEM((tm, d), F32)],
        compiler_params=pltpu.CompilerParams(
            dimension_semantics=("parallel", "arbitrary"), vmem_limit_bytes=VMEM_LIMIT),
        name="ffn_ln",
    )(x, w_in_bf, w_in_bf, w_out_bf, g.reshape(1, d), b.reshape(1, d))


def _in_proj_kernel(x_ref, wa_ref, wkw_ref, wk2_ref, wg_ref, bg_ref,
                    u_ref, qb_ref, k_ref, v_ref, kb_ref, vb_ref, qib_ref, kiwi_ref, ki2_ref, gs_ref):
    xb = x_ref[...].astype(BF16)
    glu = _dot(xb, wa_ref[:, 0:C_Q])
    u_ref[...] = glu[:, :D_CONV] * _sigmoid(glu[:, D_CONV:])
    q = _dot(xb, wa_ref[:, C_Q:C_K])
    qb_ref[...] = (q * (HEAD_DIM ** -0.5)).astype(BF16)
    k = _dot(xb, wa_ref[:, C_K:C_V])
    k_ref[...] = k
    kb_ref[...] = k.astype(BF16)
    v = _dot(xb, wa_ref[:, C_V:C_QI])
    v_ref[...] = v
    vb_ref[...] = v.astype(BF16)
    qib_ref[...] = _dot(xb, wa_ref[:, C_QI:C_KI]).astype(BF16)
    kiwi_ref[...] = _dot(xb, wkw_ref[...])
    ki2_ref[...] = _dot(xb, wk2_ref[...]).astype(BF16)
    gs_ref[...] = _sigmoid(_dot(xb, wg_ref[...]) + bg_ref[...])


def in_proj(x1, wa, wkw, wk2, wg, b_gate, *, tm):
    rows, d = x1.shape
    row_spec = lambda w: pl.BlockSpec((tm, w), lambda i: (i, 0))
    full = lambda a: pl.BlockSpec(a.shape, lambda i: (0, 0))
    bg = b_gate.reshape(1, -1)
    outs = [
        (D_CONV, F32),
        (ATT_W, BF16),
        (ATT_W, F32),
        (ATT_W, F32),
        (ATT_W, BF16),
        (ATT_W, BF16),
        (ATT_W, BF16),
        (128, F32),
        (128, BF16),
        (2 * D_MODEL, F32),
    ]
    return pl.pallas_call(
        _in_proj_kernel,
        grid=(rows // tm,),
        in_specs=[row_spec(d), full(wa), full(wkw), full(wk2), full(wg), full(bg)],
        out_specs=[row_spec(w) for w, _ in outs],
        out_shape=[jax.ShapeDtypeStruct((rows, w), dt) for w, dt in outs],
        compiler_params=pltpu.CompilerParams(
            dimension_semantics=("parallel",), vmem_limit_bytes=VMEM_LIMIT),
        name="in_proj",
    )(x1, wa, wkw, wk2, wg, bg)


CONV_HALO = 32
CONV_SUB = 64


def _conv_post(c, cb_ref, g_ref, b_ref):
    y = _layer_norm(c + cb_ref[...], g_ref[...], b_ref[...])
    return y * _sigmoid(y)


def _conv_prompt_kernel(halo_ref, cur_ref, w_ref, cb_ref, g_ref, b_ref, o_ref, ext_ref, *, tm):
    i = pl.program_id(0)

    @pl.when(i == 0)
    def _():
        ext_ref[0:CONV_HALO, :] = jnp.zeros((CONV_HALO, D_CONV), F32)

    @pl.when(i > 0)
    def _():
        ext_ref[0:CONV_HALO, :] = halo_ref[...]

    ext_ref[CONV_HALO:, :] = cur_ref[...]
    off = CONV_HALO - (CONV_W - 1)
    for r0 in range(0, tm, CONV_SUB):
        acc = jnp.zeros((CONV_SUB, D_CONV), F32)
        for j in range(CONV_W):
            acc = acc + ext_ref[r0 + off + j:r0 + off + j + CONV_SUB, :] * w_ref[j:j + 1, :]
        o_ref[r0:r0 + CONV_SUB, :] = _conv_post(acc, cb_ref, g_ref, b_ref)


def conv_prompt(u, conv_w, conv_b, g, b, *, rows, tm):
    per = tm // CONV_HALO
    vec = lambda a: a.reshape(1, D_CONV)
    full = lambda shape: pl.BlockSpec(shape, lambda i: (0, 0))
    return pl.pallas_call(
        functools.partial(_conv_prompt_kernel, tm=tm),
        grid=(rows // tm,),
        in_specs=[
            pl.BlockSpec((CONV_HALO, D_CONV), lambda i: (jnp.maximum(i * per - 1, 0), 0)),
            pl.BlockSpec((tm, D_CONV), lambda i: (i, 0)),
            full((CONV_W, D_CONV)), full((1, D_CONV)), full((1, D_CONV)), full((1, D_CONV)),
        ],
        out_specs=pl.BlockSpec((tm, D_CONV), lambda i: (i, 0)),
        out_shape=jax.ShapeDtypeStruct((rows, D_CONV), F32),
        scratch_shapes=[pltpu.VMEM((CONV_HALO + tm, D_CONV), F32)],
        compiler_params=pltpu.CompilerParams(dimension_semantics=("arbitrary",)),
        name="conv_prompt",
    )(u, u, conv_w, vec(conv_b), vec(g), vec(b))


def _conv_sample_kernel(state_ref, us_ref, w_ref, cb_ref, g_ref, b_ref, o_ref):
    acc = us_ref[...] * w_ref[CONV_W - 1:CONV_W, :]
    for j in range(CONV_W - 1):
        acc = acc + state_ref[j] * w_ref[j:j + 1, :]
    o_ref[...] = _conv_post(acc, cb_ref, g_ref, b_ref)


def conv_sample(state_t, u, conv_w, conv_b, g, b, *, row0):
    nb = state_t.shape[1]
    vec = lambda a: a.reshape(1, D_CONV)
    full = lambda shape: pl.BlockSpec(shape, lambda i: (0,) * len(shape))
    assert row0 % nb == 0
    return pl.pallas_call(
        _conv_sample_kernel,
        grid=(1,),
        in_specs=[
            full(state_t.shape),
            pl.BlockSpec((nb, D_CONV), lambda i: (row0 // nb, 0)),
            full((CONV_W, D_CONV)), full((1, D_CONV)), full((1, D_CONV)), full((1, D_CONV)),
        ],
        out_specs=full((nb, D_CONV)),
        out_shape=jax.ShapeDtypeStruct((nb, D_CONV), F32),
        name="conv_sample",
    )(state_t, u, conv_w, vec(conv_b), vec(g), vec(b))


KC = 4 * ROW_TILE
MAX_BISECTIONS = 400
NEG_INF = float("-inf")
POS_INF = float("inf")


def _lane_tiles(x):
    return [x[:, t * ROW_TILE:(t + 1) * ROW_TILE] for t in range(x.shape[1] // ROW_TILE)]


def _fold(x, op=jnp.add):
    return functools.reduce(op, _lane_tiles(x))


def _count_rows(sc_ref, nch, pred):
    def body(c, acc):
        return acc + _fold(jnp.where(pred(sc_ref[c], c), 1.0, 0.0))

    acc = lax.fori_loop(0, nch, body, jnp.zeros((ROW_TILE, ROW_TILE), F32))
    return jnp.sum(acc, axis=1, keepdims=True)


def _last_kept_position(tied_before, need, n_pos):
    nbits = (n_pos - 1).bit_length()

    def bit_body(b, j):
        cand = j | lax.shift_left(jnp.int32(1), nbits - 1 - b)
        return jnp.where(tied_before(cand) < need, cand, j)

    return lax.fori_loop(0, nbits, bit_body, jnp.zeros((ROW_TILE, 1), jnp.int32))


def _select_topk(sc_ref, nch, n_vis, topk):
    kf = float(topk)
    wide = lambda v: jnp.broadcast_to(v, (ROW_TILE, KC))

    def range_body(c, carry):
        lo, hi = carry
        sc = sc_ref[c]
        seen = jnp.where(sc == NEG_INF, POS_INF, sc)
        lo = jnp.minimum(lo, _fold(seen, jnp.minimum))
        hi = jnp.maximum(hi, _fold(sc, jnp.maximum))
        return lo, hi

    lo_t, hi_t = lax.fori_loop(
        0, nch, range_body,
        (jnp.full((ROW_TILE, ROW_TILE), POS_INF, F32), jnp.full((ROW_TILE, ROW_TILE), NEG_INF, F32)))
    rmin = jnp.min(lo_t, axis=1, keepdims=True)
    rmax = jnp.max(hi_t, axis=1, keepdims=True)
    needs = n_vis > kf

    def any_row(flag):
        return jnp.max(jnp.where(flag, 1.0, 0.0))

    def zero_body(c, carry):
        ge, gt = carry
        sc = sc_ref[c]
        return ge + _fold(jnp.where(sc >= 0.0, 1.0, 0.0)), gt + _fold(jnp.where(sc > 0.0, 1.0, 0.0))

    ge_t, gt_t = lax.fori_loop(0, nch, zero_body, (jnp.zeros((ROW_TILE, ROW_TILE), F32),) * 2)
    cnt_ge0 = jnp.sum(ge_t, axis=1, keepdims=True)
    cnt_gt0 = jnp.sum(gt_t, axis=1, keepdims=True)
    above = cnt_gt0 >= kf
    below = cnt_ge0 < kf
    zero_tied = needs & jnp.logical_not(above) & jnp.logical_not(below)

    def smallest_positive():
        def body(c, acc):
            sc = sc_ref[c]
            return jnp.minimum(acc, _fold(jnp.where(sc > 0.0, sc, POS_INF), jnp.minimum))

        acc = lax.fori_loop(0, nch, body, jnp.full((ROW_TILE, ROW_TILE), POS_INF, F32))
        return jnp.min(acc, axis=1, keepdims=True)

    min_pos = lax.cond(any_row(zero_tied) > 0.0, smallest_positive,
                       lambda: jnp.full((ROW_TILE, 1), POS_INF, F32))

    lo0 = jnp.where(below, rmin, 0.0)
    cnt_lo0 = jnp.where(below, n_vis, cnt_ge0)
    hi_top = rmax + (0.5 * (jnp.abs(rmax) + jnp.abs(rmin)) + 1e-30)
    hi0 = jnp.where(above, hi_top, jnp.where(zero_tied, min_pos, 0.0))
    cnt_hi0 = jnp.where(above, 0.0, jnp.where(zero_tied, cnt_gt0, cnt_ge0))
    search = needs & jnp.logical_not(zero_tied)

    def searching(lo, hi, cnt_lo):
        mid = lo + 0.5 * (hi - lo)
        return mid, search & (cnt_lo != kf) & (mid > lo) & (mid < hi)

    def cond(st):
        return (st[0] < MAX_BISECTIONS) & (st[1] > 0.0)

    def body(st):
        it, _, lo, hi, cnt_lo, cnt_hi = st
        mid, open_ = searching(lo, hi, cnt_lo)
        mid_w = wide(mid)
        cnt = _count_rows(sc_ref, nch, lambda sc, c: sc >= mid_w)
        up = open_ & (cnt >= kf)
        dn = open_ & (cnt < kf)
        lo = jnp.where(up, mid, lo)
        cnt_lo = jnp.where(up, cnt, cnt_lo)
        hi = jnp.where(dn, mid, hi)
        cnt_hi = jnp.where(dn, cnt, cnt_hi)
        _, still = searching(lo, hi, cnt_lo)
        return it + 1, any_row(still), lo, hi, cnt_lo, cnt_hi

    _, open0 = searching(lo0, hi0, cnt_lo0)
    _, _, lo, hi, cnt_lo, cnt_hi = lax.while_loop(
        cond, body, (jnp.int32(0), any_row(open0), lo0, hi0, cnt_lo0, cnt_hi0))
    thr = jnp.where(needs, lo, rmin)

    surplus = needs & (cnt_lo > kf)

    @pl.when(any_row(surplus) > 0.0)
    def _():
        lo_w, hi_w, surplus_w = wide(lo), wide(hi), wide(surplus)
        lane = lax.broadcasted_iota(jnp.int32, (ROW_TILE, KC), 1)
        need = kf - cnt_hi

        def tied_before(bound):
            bound_w = wide(bound)
            return _count_rows(
                sc_ref, nch, lambda sc, c: (sc >= lo_w) & (sc < hi_w) & (lane + c * KC < bound_w))

        last_w = wide(_last_kept_position(tied_before, need, sc_ref.shape[0] * KC))

        def demote(c, carry):
            sc = sc_ref[c]
            drop = surplus_w & (sc >= lo_w) & (sc < hi_w) & (lane + c * KC > last_w)
            sc_ref[c] = jnp.where(drop, NEG_INF, sc)
            return carry

        lax.fori_loop(0, nch, demote, 0)

    return thr


def _prompt_attn_kernel(qi_ref, kiwi_ref, q_ref, ki2_ref, k_ref, v_ref, o_ref,
                        sc_ref, m_ref, l_ref, acc_ref, p_ref, *, topk):
    i = pl.program_id(0)
    nch = (i * ROW_TILE) // KC + 1
    lane = lax.broadcasted_iota(jnp.int32, (ROW_TILE, ROW_TILE), 1)
    lo_half = lane < HEAD_DIM
    qpos = i * ROW_TILE + lax.broadcasted_iota(jnp.int32, (ROW_TILE, KC), 0)
    kiota = lax.broadcasted_iota(jnp.int32, (ROW_TILE, KC), 1)

    def head_lanes(ref, h):
        p = h // 2
        blk = ref[:, p * ROW_TILE:(p + 1) * ROW_TILE]
        keep = lo_half if h % 2 == 0 else jnp.logical_not(lo_half)
        return jnp.where(keep, blk, jnp.zeros_like(blk))

    qis = [head_lanes(qi_ref, h) for h in range(N_HEADS)]
    wi = kiwi_ref[...]
    wis = [jnp.broadcast_to(wi[:, D_IDX + h:D_IDX + h + 1], (ROW_TILE, ROW_TILE)) for h in range(N_HEADS)]

    def score_body(c, carry):
        k0 = pl.multiple_of(c * KC, KC)
        kic = ki2_ref[pl.ds(k0, KC), :]
        parts = [jnp.zeros((ROW_TILE, ROW_TILE), F32)] * (KC // ROW_TILE)
        for h in range(N_HEADS):
            s = jnp.maximum(_nt_dot(qis[h], kic), 0.0)
            parts = [acc + tile * wis[h] for acc, tile in zip(parts, _lane_tiles(s))]
        sc = jnp.concatenate(parts, axis=1)
        sc_ref[c] = jnp.where(kiota + k0 <= qpos, sc, NEG_INF)
        return carry

    lax.fori_loop(0, nch, score_body, 0)

    n_vis = (qpos[:, 0:1] + 1).astype(F32)
    thr_w = jnp.broadcast_to(_select_topk(sc_ref, nch, n_vis, topk), (ROW_TILE, KC))

    qs = [head_lanes(q_ref, h) for h in range(N_HEADS)]
    m_ref[...] = jnp.full(m_ref.shape, NEG_BIG, F32)

    def masked_scores(c):
        k0 = pl.multiple_of(c * KC, KC)
        bias = jnp.where(sc_ref[c] >= thr_w, 0.0, NEG_BIG)
        return k0, bias

    def max_body(c, carry):
        k0, bias = masked_scores(c)
        for p in range(N_HEADS // 2):
            kp = k_ref[pl.ds(k0, KC), p * ROW_TILE:(p + 1) * ROW_TILE]
            for h in (2 * p, 2 * p + 1):
                s = _nt_dot(qs[h], kp) + bias
                m_ref[h] = jnp.maximum(m_ref[h], _fold(s, jnp.maximum))
        return carry

    lax.fori_loop(0, nch, max_body, 0)
    for h in range(N_HEADS):
        m_ref[h] = jnp.broadcast_to(jnp.max(m_ref[h], axis=1, keepdims=True), (ROW_TILE, ROW_TILE))

    l_ref[...] = jnp.zeros(l_ref.shape, F32)
    acc_ref[...] = jnp.zeros(acc_ref.shape, F32)

    p_ref[...] = jnp.zeros(p_ref.shape, BF16)

    def weighted_values(c, slot):
        k0 = pl.multiple_of(c * KC, KC)
        for p in range(N_HEADS // 2):
            vp = v_ref[pl.ds(k0, KC), p * ROW_TILE:(p + 1) * ROW_TILE]
            for h in (2 * p, 2 * p + 1):
                acc_ref[h] += _dot(p_ref[slot, h], vp)

    def pv_body(c, carry):
        slot = c % 2
        k0, bias = masked_scores(c)
        for p in range(N_HEADS // 2):
            kp = k_ref[pl.ds(k0, KC), p * ROW_TILE:(p + 1) * ROW_TILE]
            for h in (2 * p, 2 * p + 1):
                s = _nt_dot(qs[h], kp) + bias
                m = m_ref[h]
                es = [jnp.exp(tile - m) for tile in _lane_tiles(s)]
                l_ref[h] += functools.reduce(jnp.add, es)
                p_ref[slot, h] = jnp.concatenate(es, axis=1).astype(BF16)
        weighted_values(jnp.maximum(c - 1, 0), 1 - slot)
        return carry

    lax.fori_loop(0, nch, pv_body, 0)
    weighted_values(nch - 1, (nch - 1) % 2)

    for p in range(N_HEADS // 2):
        even = acc_ref[2 * p] / jnp.sum(l_ref[2 * p], axis=1, keepdims=True)
        odd = acc_ref[2 * p + 1] / jnp.sum(l_ref[2 * p + 1], axis=1, keepdims=True)
        o_ref[:, p * ROW_TILE:(p + 1) * ROW_TILE] = jnp.where(lo_half, even, odd)


def prompt_attention(qi_bf, kiwi, q_bf, ki2_bf, k_bf, v_bf, *, n_blocks, topk):
    rows = k_bf.shape[0]
    blk = lambda w: pl.BlockSpec((ROW_TILE, w), lambda i: (i, 0))
    full = lambda a: pl.BlockSpec(a.shape, lambda i: (0, 0))
    head_tiles = pltpu.VMEM((N_HEADS, ROW_TILE, ROW_TILE), F32)
    return pl.pallas_call(
        functools.partial(_prompt_attn_kernel, topk=topk),
        grid=(n_blocks,),
        in_specs=[blk(ATT_W), blk(128), blk(ATT_W), full(ki2_bf), full(k_bf), full(v_bf)],
        out_specs=blk(ATT_W),
        out_shape=jax.ShapeDtypeStruct((n_blocks * ROW_TILE, ATT_W), F32),
        scratch_shapes=[pltpu.VMEM((rows // KC, ROW_TILE, KC), F32), head_tiles, head_tiles, head_tiles,
                        pltpu.VMEM((2, N_HEADS, ROW_TILE, KC), BF16)],
        compiler_params=pltpu.CompilerParams(
            dimension_semantics=("arbitrary",), vmem_limit_bytes=VMEM_LIMIT),
        name="prompt_attn",
    )(qi_bf, kiwi, q_bf, ki2_bf, k_bf, v_bf)


HEAD_ROWS = 16
PAGE_GROUP = 8
PAGE_SLOTS = 8


def _sample_index_kernel(pt_ref, qi_ref, wi_ref, kin_ref, cik_ref, sc_out, thr_out,
                         kibuf, sc_ref, sem, *, topk, n_pages):
    b = pl.program_id(0)

    def page_copy(pg):
        return pltpu.make_async_copy(cik_ref.at[pt_ref[b * n_pages + pg]], kibuf.at[pg], sem.at[0])

    def start(pg, carry):
        page_copy(pg).start()
        return carry

    def wait(pg, carry):
        page_copy(pg).wait()
        return carry

    lax.fori_loop(0, n_pages, start, 0)
    lax.fori_loop(0, n_pages, wait, 0)

    qi = qi_ref[0]
    wi = wi_ref[0]
    pages_per_chunk = KC // PAGE_SIZE
    for c in range(n_pages // pages_per_chunk):
        parts = []
        for t in range(pages_per_chunk):
            kit = kibuf[c * pages_per_chunk + t].astype(BF16)
            s = jnp.maximum(_dot(qi, kit), 0.0) * wi
            parts.append(jnp.sum(s, axis=0, keepdims=True))
        sc_ref[c, pl.ds(b, 1), :] = jnp.concatenate(parts, axis=1)

    kin = kin_ref[0].astype(BF16).astype(F32)
    s_new = jnp.sum(qi.astype(F32) * kin, axis=1, keepdims=True)
    sc_new = jnp.sum(jnp.maximum(s_new, 0.0) * wi, axis=0, keepdims=True)
    lane = lax.broadcasted_iota(jnp.int32, (1, KC), 1)
    new_chunk = n_pages // pages_per_chunk
    sc_ref[new_chunk, pl.ds(b, 1), :] = jnp.where(lane == 0, sc_new, NEG_INF)

    @pl.when(b == pl.num_programs(0) - 1)
    def _():
        nch = new_chunk + 1
        n_vis = jnp.full((ROW_TILE, 1), float(n_pages * PAGE_SIZE + 1), F32)
        thr = _select_topk(sc_ref, nch, n_vis, topk)
        thr_out[...] = jnp.broadcast_to(thr, (ROW_TILE, ROW_TILE))
        for c in range(nch):
            sc_out[:, c * KC:(c + 1) * KC] = sc_ref[c]


def sample_index(page_table, qi_s, wi_s, ki_new, cache_ik_t, *, topk):
    nb, n_pages = page_table.shape
    assert nb == ROW_TILE and cache_ik_t.shape[1:] == (D_IDX, PAGE_SIZE)
    nch = n_pages * PAGE_SIZE // KC + 1
    per_q = lambda shape: pl.BlockSpec((1,) + shape, lambda b, pt: (b, 0, 0))
    full = lambda shape: pl.BlockSpec(shape, lambda b, pt: (0, 0))
    grid_spec = pltpu.PrefetchScalarGridSpec(
        num_scalar_prefetch=1,
        grid=(nb,),
        in_specs=[per_q((HEAD_ROWS, D_IDX)), per_q((HEAD_ROWS, 1)), per_q((1, D_IDX)),
                  pl.BlockSpec(memory_space=pl.ANY)],
        out_specs=[full((nb, nch * KC)), full((nb, ROW_TILE))],
        scratch_shapes=[
            pltpu.VMEM((n_pages, D_IDX, PAGE_SIZE), F32),
            pltpu.VMEM((nch, ROW_TILE, KC), F32),
            pltpu.SemaphoreType.DMA((1,)),
        ],
    )
    return pl.pallas_call(
        functools.partial(_sample_index_kernel, topk=topk, n_pages=n_pages),
        grid_spec=grid_spec,
        out_shape=[jax.ShapeDtypeStruct((nb, nch * KC), F32), jax.ShapeDtypeStruct((nb, ROW_TILE), F32)],
        compiler_params=pltpu.CompilerParams(
            dimension_semantics=("arbitrary",), vmem_limit_bytes=VMEM_LIMIT),
        name="sample_index",
    )(page_table.reshape(-1), qi_s, wi_s, ki_new, cache_ik_t)


def _sample_attn_kernel(pt_ref, bdq_ref, sc_ref, thr_ref, kn_ref, vn_ref, ck_ref, cv_ref, o_ref,
                        buf, s_ref, sem, *, n_pages):
    b = pl.program_id(0)
    past = n_pages * PAGE_SIZE
    units_per_cache = n_pages // PAGE_GROUP
    n_units = 2 * units_per_cache

    def unit_copies(q, t):
        cache = ck_ref if t < units_per_cache else cv_ref
        first = (t % units_per_cache) * PAGE_GROUP
        slot = t % PAGE_SLOTS
        return [pltpu.make_async_copy(cache.at[pt_ref[q * n_pages + first + j]], buf.at[slot, j], sem.at[slot])
                for j in range(PAGE_GROUP)]

    def start_unit(q, t):
        for cp in unit_copies(q, t):
            cp.start()

    ahead = PAGE_SLOTS - 1

    @pl.when(b == 0)
    def _():
        for t in range(ahead):
            start_unit(b, t)

    bdq = bdq_ref[0]
    out = jnp.zeros((HEAD_ROWS, ATT_W), F32)
    denom = None
    for t in range(n_units):
        for cp in unit_copies(b, t):
            cp.wait()
        if t + ahead < n_units:
            start_unit(b, t + ahead)
        else:
            @pl.when(b + 1 < pl.num_programs(0))
            def _(t=t):
                start_unit(b + 1, t + ahead - n_units)

        slot = t % PAGE_SLOTS
        for j in range(PAGE_GROUP):
            pg = (t % units_per_cache) * PAGE_GROUP + j
            lanes = slice(pg * PAGE_SIZE, (pg + 1) * PAGE_SIZE)
            page = buf[slot, j].reshape(ATT_W, PAGE_SIZE).astype(BF16)
            if t < units_per_cache:
                s_ref[:, lanes] = _dot(bdq, page)
            else:
                out = out + _nt_dot(s_ref[:, lanes].astype(BF16), page)

        if t == units_per_cache - 1:
            sc = sc_ref[0]
            thr = thr_ref[0][:, 0:1]
            sel = sc[:, :past] >= thr
            sel_new = sc[:, past:past + 1] >= thr
            s = s_ref[...] + jnp.where(sel, 0.0, NEG_BIG)
            kn = kn_ref[0].astype(BF16).astype(F32)
            s_new = jnp.sum(bdq.astype(F32) * kn, axis=1, keepdims=True)
            s_new = jnp.where(sel_new, s_new, NEG_BIG)
            m = jnp.maximum(jnp.max(s, axis=1, keepdims=True), s_new)
            pr = jnp.exp(s - m)
            p_new = jnp.where(sel_new, jnp.exp(s_new - m), 0.0)
            denom = jnp.sum(pr, axis=1, keepdims=True) + p_new
            s_ref[...] = pr
            vn = vn_ref[0].astype(BF16).astype(F32)
            out = p_new.astype(BF16).astype(F32) * vn

    out = out / denom
    head_of_lane = lax.broadcasted_iota(jnp.int32, (HEAD_ROWS, ATT_W), 1) // HEAD_DIM
    head_of_row = lax.broadcasted_iota(jnp.int32, (HEAD_ROWS, ATT_W), 0)
    o_ref[0] = jnp.sum(jnp.where(head_of_lane == head_of_row, out, 0.0), axis=0, keepdims=True)


def sample_attention(page_table, bdq, sc_all, thr, k_new, v_new, cache_k_t, cache_v_t):
    nb, n_pages = page_table.shape
    assert cache_k_t.shape[1:] == (N_HEADS, HEAD_DIM, PAGE_SIZE)
    assert (2 * n_pages // PAGE_GROUP) % PAGE_SLOTS == 0
    past = n_pages * PAGE_SIZE
    per_q = lambda shape: pl.BlockSpec((1,) + shape, lambda b, pt: (b, 0, 0))
    any_spec = pl.BlockSpec(memory_space=pl.ANY)
    grid_spec = pltpu.PrefetchScalarGridSpec(
        num_scalar_prefetch=1,
        grid=(nb,),
        in_specs=[per_q((HEAD_ROWS, ATT_W)), per_q((1, sc_all.shape[-1])), per_q((1, ROW_TILE)),
                  per_q((1, ATT_W)), per_q((1, ATT_W)), any_spec, any_spec],
        out_specs=per_q((1, ATT_W)),
        scratch_shapes=[
            pltpu.VMEM((PAGE_SLOTS, PAGE_GROUP, N_HEADS, HEAD_DIM, PAGE_SIZE), F32),
            pltpu.VMEM((HEAD_ROWS, past), F32),
            pltpu.SemaphoreType.DMA((PAGE_SLOTS,)),
        ],
    )
    return pl.pallas_call(
        functools.partial(_sample_attn_kernel, n_pages=n_pages),
        grid_spec=grid_spec,
        out_shape=jax.ShapeDtypeStruct((nb, 1, ATT_W), F32),
        compiler_params=pltpu.CompilerParams(
            dimension_semantics=("arbitrary",), vmem_limit_bytes=VMEM_LIMIT),
        name="sample_attn",
    )(page_table.reshape(-1), bdq, sc_all.reshape(nb, 1, -1), thr.reshape(nb, 1, ROW_TILE),
      k_new, v_new, cache_k_t, cache_v_t)


def _merge_ln_kernel(uc_ref, att_ref, gs_ref, x1_ref, wb0_ref, wb1_ref, wo_ref, g_ref, b_ref, o_ref):
    conv_br = _dot(uc_ref[...].astype(BF16), wb0_ref[...])
    att_br = _dot(att_ref[...].astype(BF16), wb1_ref[...])
    m = gs_ref[:, :D_MODEL] * conv_br + gs_ref[:, D_MODEL:] * att_br
    h = _dot(m.astype(BF16), wo_ref[...])
    o_ref[...] = _layer_norm(ALPHA * x1_ref[...] + h, g_ref[...], b_ref[...])


def merge_ln(u_conv, att, gs, x1, wb0, wb1, wo, g, b, *, tm):
    rows, d = x1.shape
    row_spec = lambda w: pl.BlockSpec((tm, w), lambda i: (i, 0))
    full = lambda a: pl.BlockSpec(a.shape, lambda i: (0, 0))
    g2, b2 = g.reshape(1, d), b.reshape(1, d)
    return pl.pallas_call(
        _merge_ln_kernel,
        grid=(rows // tm,),
        in_specs=[row_spec(D_CONV), row_spec(ATT_W), row_spec(2 * d), row_spec(d),
                  full(wb0), full(wb1), full(wo), full(g2), full(b2)],
        out_specs=row_spec(d),
        out_shape=jax.ShapeDtypeStruct((rows, d), F32),
        compiler_params=pltpu.CompilerParams(
            dimension_semantics=("parallel",), vmem_limit_bytes=VMEM_LIMIT),
        name="merge_ln",
    )(u_conv, att, gs, x1, wb0, wb1, wo, g2, b2)


def kernel(x_prompt, x_sample, cache_k, cache_v, cache_idx_k, state_conv, page_table, meta_tokens,
           ffn1_w_in, ffn1_w_out, ln1_g, ln1_b, w_in, conv_w, conv_b, conv_ln_g, conv_ln_b, b_gate,
           w_branch, w_out, ln2_g, ln2_b, ffn2_w_in, ffn2_w_out, ln3_g, ln3_b):
    topk_prompt = min(TOPK, SEQ // 4)
    topk_sample = min(TOPK, (PAST_LEN + 1) // 4)

    x_all = jnp.concatenate([
        meta_tokens.astype(F32), x_prompt[0],
        jnp.zeros((LP - L_PROMPT, D_MODEL), F32), x_sample[:, 0],
        jnp.zeros((R_ALL - S_ROW1, D_MODEL), F32)], axis=0)
    wa = w_in[:, :C_KI].astype(BF16)
    w_ki = w_in[:, C_KI:C_WI]
    wkw = jnp.concatenate([w_in[:, C_KI:C_G], jnp.zeros((D_MODEL, 128 - D_IDX - N_HEADS), F32)],
                          axis=1).astype(BF16)
    wk2 = jnp.concatenate([w_ki, w_ki], axis=1).astype(BF16)
    wg = w_in[:, C_G:].astype(BF16)

    x1 = ffn_ln(x_all, ffn1_w_in.astype(BF16), ffn1_w_out.astype(BF16), ln1_g, ln1_b, tm=512, tf=D_FF // 2)
    u, q_bf, k, v, k_bf, v_bf, qi_bf, kiwi, ki2_bf, gs = in_proj(x1, wa, wkw, wk2, wg, b_gate, tm=384)
    sample_rows = lambda a: a[S_ROW0:S_ROW1]
    tail_rows = jnp.zeros((R_ALL - S_ROW1, ATT_W), F32)

    uc_prompt = conv_prompt(u, conv_w, conv_b, conv_ln_g, conv_ln_b, rows=LP, tm=384)
    uc_sample = conv_sample(jnp.transpose(state_conv, (1, 0, 2)), u, conv_w, conv_b, conv_ln_g, conv_ln_b,
                            row0=S_ROW0)
    u_conv = jnp.concatenate([uc_prompt, uc_sample, tail_rows], axis=0)

    att_prompt = prompt_attention(qi_bf, kiwi, q_bf, ki2_bf, k_bf, v_bf,
                                  n_blocks=LP // ROW_TILE, topk=topk_prompt)
    pad_heads = lambda a: jnp.pad(a, ((0, 0), (0, HEAD_ROWS - N_HEADS), (0, 0)))
    kiwi_s, k_s, v_s, u_s = sample_rows(kiwi), sample_rows(k), sample_rows(v), sample_rows(u)
    qi_s = pad_heads(sample_rows(qi_bf).reshape(DEC_BATCH, N_HEADS, D_IDX))
    wi_s = pad_heads(kiwi_s[:, D_IDX:D_IDX + N_HEADS].reshape(DEC_BATCH, N_HEADS, 1))
    head_of_lane = jnp.arange(ATT_W) // HEAD_DIM
    bdq = jnp.where(head_of_lane[None, None, :] == jnp.arange(HEAD_ROWS)[None, :, None],
                    sample_rows(q_bf)[:, None, :], jnp.zeros((), BF16))
    cache_ik_t = jnp.transpose(cache_idx_k, (0, 2, 1))
    cache_k_t = jnp.transpose(cache_k, (0, 2, 3, 1))
    cache_v_t = jnp.transpose(cache_v, (0, 2, 3, 1))
    sc_all, thr = sample_index(page_table, qi_s, wi_s, kiwi_s[:, :D_IDX].reshape(DEC_BATCH, 1, D_IDX),
                               cache_ik_t, topk=topk_sample)
    att_sample = sample_attention(
        page_table, bdq, sc_all, thr,
        k_s.reshape(DEC_BATCH, 1, ATT_W), v_s.reshape(DEC_BATCH, 1, ATT_W), cache_k_t, cache_v_t)
    att = jnp.concatenate([att_prompt, att_sample[:, 0], tail_rows], axis=0)

    x2 = merge_ln(u_conv, att, gs, x1, w_branch[0].astype(BF16), w_branch[1].astype(BF16),
                  w_out.astype(BF16), ln2_g, ln2_b, tm=512)
    y = ffn_ln(x2, ffn2_w_in.astype(BF16), ffn2_w_out.astype(BF16), ln3_g, ln3_b, tm=512, tf=D_FF // 2)

    hd = (N_HEADS, HEAD_DIM)
    y_prompt = y[N_META:L_PROMPT][None]
    y_sample = sample_rows(y)[:, None]
    k_prompt = k[:L_PROMPT].reshape((1, L_PROMPT) + hd)
    v_prompt = v[:L_PROMPT].reshape((1, L_PROMPT) + hd)
    idxk_prompt = kiwi[:L_PROMPT, :D_IDX][None]
    conv_prompt_state = u[L_PROMPT - (CONV_W - 1):L_PROMPT][None]
    k_sample = k_s.reshape((DEC_BATCH, 1) + hd)
    v_sample = v_s.reshape((DEC_BATCH, 1) + hd)
    idxk_sample = kiwi_s[:, :D_IDX][:, None]
    conv_sample_state = jnp.concatenate([state_conv[:, 1:].astype(F32), u_s[:, None]], axis=1)
    return (y_prompt, y_sample, k_prompt, v_prompt, idxk_prompt, conv_prompt_state,
            k_sample, v_sample, idxk_sample, conv_sample_state)
```

```python
import functools

import jax
import jax.numpy as jnp
from jax import lax
from jax.experimental import pallas as pl
from jax.experimental.pallas import tpu as pltpu

D_MODEL = 1024
SEQ = 16384
N_META = 16
L_PROMPT = SEQ + N_META
DEC_BATCH = 128
PAST_LEN = 8192
PAGE_SIZE = 128
N_PAGES = PAST_LEN // PAGE_SIZE
D_CONV = 512
CONV_W = 31
N_HEADS = 8
HEAD_DIM = 64
ATT_W = N_HEADS * HEAD_DIM
D_IDX = 64
TOPK = 256
D_FF = 2816
ALPHA = 2.0 ** 0.25
LN_EPS = 1e-5

ROW_TILE = 128
LP = 16512
S_ROW0 = LP
S_ROW1 = S_ROW0 + DEC_BATCH
R_ALL = 17408

C_Q = 2 * D_CONV
C_K = C_Q + ATT_W
C_V = C_K + ATT_W
C_QI = C_V + ATT_W
C_KI = C_QI + N_HEADS * D_IDX
C_WI = C_KI + D_IDX
C_G = C_WI + N_HEADS

NEG_BIG = -1e30
VMEM_LIMIT = 60 * 1024 * 1024

BF16 = jnp.bfloat16
F32 = jnp.float32


def _nt_dot(a, b):
    return lax.dot_general(a, b, (((1,), (1,)), ((), ())), preferred_element_type=F32)


def _dot(a, b):
    return jnp.dot(a, b, preferred_element_type=F32)


def _sigmoid(x):
    return 1.0 / (1.0 + jnp.exp(-x))


def _layer_norm(x, g, b):
    mu = jnp.mean(x, axis=-1, keepdims=True)
    xc = x - mu
    var = jnp.mean(xc * xc, axis=-1, keepdims=True)
    return xc * lax.rsqrt(var + LN_EPS) * g + b


def _ffn_ln_kernel(x_ref, wg_ref, wu_ref, wo_ref, g_ref, b_ref, o_ref, acc_ref):
    j = pl.program_id(1)
    xb = x_ref[...].astype(BF16)
    gate = _dot(xb, wg_ref[...])
    up = _dot(xb, wu_ref[...])
    h = (gate * _sigmoid(gate)) * up
    part = _dot(h.astype(BF16), wo_ref[...])

    @pl.when(j == 0)
    def _():
        acc_ref[...] = part

    @pl.when(j > 0)
    def _():
        acc_ref[...] += part

    @pl.when(j == pl.num_programs(1) - 1)
    def _():
        y = ALPHA * x_ref[...] + 0.5 * acc_ref[...]
        o_ref[...] = _layer_norm(y, g_ref[...], b_ref[...])


def ffn_ln(x, w_in_bf, w_out_bf, g, b, *, tm, tf):
    rows, d = x.shape
    d_ff = w_out_bf.shape[0]
    nf = d_ff // tf
    return pl.pallas_call(
        _ffn_ln_kernel,
        grid=(rows // tm, nf),
        in_specs=[
            pl.BlockSpec((tm, d), lambda i, j: (i, 0)),
            pl.BlockSpec((d, tf), lambda i, j: (0, j)),
            pl.BlockSpec((d, tf), lambda i, j, nf=nf: (0, j + nf)),
            pl.BlockSpec((tf, d), lambda i, j: (j, 0)),
            pl.BlockSpec((1, d), lambda i, j: (0, 0)),
            pl.BlockSpec((1, d), lambda i, j: (0, 0)),
        ],
        out_specs=pl.BlockSpec((tm, d), lambda i, j: (i, 0)),
        out_shape=jax.ShapeDtypeStruct((rows, d), F32),
        scratch_shapes=[pltpu.VMEM((tm, d), F32)],
        compiler_params=pltpu.CompilerParams(
            dimension_semantics=("parallel", "arbitrary"), vmem_limit_bytes=VMEM_LIMIT),
        name="ffn_ln",
    )(x, w_in_bf, w_in_bf, w_out_bf, g.reshape(1, d), b.reshape(1, d))


def _in_proj_kernel(x_ref, wa_ref, wkw_ref, wk2_ref, wg_ref, bg_ref,
                    u_ref, qb_ref, k_ref, v_ref, kb_ref, vb_ref, qib_ref, kiwi_ref, ki2_ref, gs_ref):
    xb = x_ref[...].astype(BF16)
    glu = _dot(xb, wa_ref[:, 0:C_Q])
    u_ref[...] = glu[:, :D_CONV] * _sigmoid(glu[:, D_CONV:])
    q = _dot(xb, wa_ref[:, C_Q:C_K])
    qb_ref[...] = (q * (HEAD_DIM ** -0.5)).astype(BF16)
    k = _dot(xb, wa_ref[:, C_K:C_V])
    k_ref[...] = k
    kb_ref[...] = k.astype(BF16)
    v = _dot(xb, wa_ref[:, C_V:C_QI])
    v_ref[...] = v
    vb_ref[...] = v.astype(BF16)
    qib_ref[...] = _dot(xb, wa_ref[:, C_QI:C_KI]).astype(BF16)
    kiwi_ref[...] = _dot(xb, wkw_ref[...])
    ki2_ref[...] = _dot(xb, wk2_ref[...]).astype(BF16)
    gs_ref[...] = _sigmoid(_dot(xb, wg_ref[...]) + bg_ref[...])


def in_proj(x1, wa, wkw, wk2, wg, b_gate, *, tm):
    rows, d = x1.shape
    row_spec = lambda w: pl.BlockSpec((tm, w), lambda i: (i, 0))
    full = lambda a: pl.BlockSpec(a.shape, lambda i: (0, 0))
    bg = b_gate.reshape(1, -1)
    outs = [
        (D_CONV, F32),
        (ATT_W, BF16),
        (ATT_W, F32),
        (ATT_W, F32),
        (ATT_W, BF16),
        (ATT_W, BF16),
        (ATT_W, BF16),
        (128, F32),
        (128, BF16),
        (2 * D_MODEL, F32),
    ]
    return pl.pallas_call(
        _in_proj_kernel,
        grid=(rows // tm,),
        in_specs=[row_spec(d), full(wa), full(wkw), full(wk2), full(wg), full(bg)],
        out_specs=[row_spec(w) for w, _ in outs],
        out_shape=[jax.ShapeDtypeStruct((rows, w), dt) for w, dt in outs],
        compiler_params=pltpu.CompilerParams(
            dimension_semantics=("parallel",), vmem_limit_bytes=VMEM_LIMIT),
        name="in_proj",
    )(x1, wa, wkw, wk2, wg, bg)


CONV_HALO = 32
CONV_SUB = 64


def _conv_post(c, cb_ref, g_ref, b_ref):
    y = _layer_norm(c + cb_ref[...], g_ref[...], b_ref[...])
    return y * _sigmoid(y)


def _conv_prompt_kernel(halo_ref, cur_ref, w_ref, cb_ref, g_ref, b_ref, o_ref, ext_ref, *, tm):
    i = pl.program_id(0)

    @pl.when(i == 0)
    def _():
        ext_ref[0:CONV_HALO, :] = jnp.zeros((CONV_HALO, D_CONV), F32)

    @pl.when(i > 0)
    def _():
        ext_ref[0:CONV_HALO, :] = halo_ref[...]

    ext_ref[CONV_HALO:, :] = cur_ref[...]
    off = CONV_HALO - (CONV_W - 1)
    for r0 in range(0, tm, CONV_SUB):
        acc = jnp.zeros((CONV_SUB, D_CONV), F32)
        for j in range(CONV_W):
            acc = acc + ext_ref[r0 + off + j:r0 + off + j + CONV_SUB, :] * w_ref[j:j + 1, :]
        o_ref[r0:r0 + CONV_SUB, :] = _conv_post(acc, cb_ref, g_ref, b_ref)


def conv_prompt(u, conv_w, conv_b, g, b, *, rows, tm):
    per = tm // CONV_HALO
    vec = lambda a: a.reshape(1, D_CONV)
    full = lambda shape: pl.BlockSpec(shape, lambda i: (0, 0))
    return pl.pallas_call(
        functools.partial(_conv_prompt_kernel, tm=tm),
        grid=(rows // tm,),
        in_specs=[
            pl.BlockSpec((CONV_HALO, D_CONV), lambda i: (jnp.maximum(i * per - 1, 0), 0)),
            pl.BlockSpec((tm, D_CONV), lambda i: (i, 0)),
            full((CONV_W, D_CONV)), full((1, D_CONV)), full((1, D_CONV)), full((1, D_CONV)),
        ],
        out_specs=pl.BlockSpec((tm, D_CONV), lambda i: (i, 0)),
        out_shape=jax.ShapeDtypeStruct((rows, D_CONV), F32),
        scratch_shapes=[pltpu.VMEM((CONV_HALO + tm, D_CONV), F32)],
        compiler_params=pltpu.CompilerParams(dimension_semantics=("arbitrary",)),
        name="conv_prompt",
    )(u, u, conv_w, vec(conv_b), vec(g), vec(b))


def _conv_sample_kernel(state_ref, us_ref, w_ref, cb_ref, g_ref, b_ref, o_ref):
    acc = us_ref[...] * w_ref[CONV_W - 1:CONV_W, :]
    for j in range(CONV_W - 1):
        acc = acc + state_ref[j] * w_ref[j:j + 1, :]
    o_ref[...] = _conv_post(acc, cb_ref, g_ref, b_ref)


def conv_sample(state_t, u, conv_w, conv_b, g, b, *, row0):
    nb = state_t.shape[1]
    vec = lambda a: a.reshape(1, D_CONV)
    full = lambda shape: pl.BlockSpec(shape, lambda i: (0,) * len(shape))
    assert row0 % nb == 0
    return pl.pallas_call(
        _conv_sample_kernel,
        grid=(1,),
        in_specs=[
            full(state_t.shape),
            pl.BlockSpec((nb, D_CONV), lambda i: (row0 // nb, 0)),
            full((CONV_W, D_CONV)), full((1, D_CONV)), full((1, D_CONV)), full((1, D_CONV)),
        ],
        out_specs=full((nb, D_CONV)),
        out_shape=jax.ShapeDtypeStruct((nb, D_CONV), F32),
        name="conv_sample",
    )(state_t, u, conv_w, vec(conv_b), vec(g), vec(b))


KC = 8 * ROW_TILE
MAX_BISECTIONS = 400
NEG_INF = float("-inf")
POS_INF = float("inf")


def _lane_tiles(x):
    return [x[:, t * ROW_TILE:(t + 1) * ROW_TILE] for t in range(x.shape[1] // ROW_TILE)]


def _fold(x, op=jnp.add):
    return functools.reduce(op, _lane_tiles(x))


def _count_rows(sc_ref, nch, pred):
    def body(c, acc):
        return acc + _fold(jnp.where(pred(sc_ref[c], c), 1.0, 0.0))

    acc = lax.fori_loop(0, nch, body, jnp.zeros((ROW_TILE, ROW_TILE), F32))
    return jnp.sum(acc, axis=1, keepdims=True)


def _last_kept_position(tied_before, need, n_pos):
    nbits = (n_pos - 1).bit_length()

    def bit_body(b, j):
        cand = j | lax.shift_left(jnp.int32(1), nbits - 1 - b)
        return jnp.where(tied_before(cand) < need, cand, j)

    return lax.fori_loop(0, nbits, bit_body, jnp.zeros((ROW_TILE, 1), jnp.int32))


def _stats_init(stat_ref):
    stat_ref[0] = jnp.full((ROW_TILE, ROW_TILE), POS_INF, F32)
    stat_ref[1] = jnp.full((ROW_TILE, ROW_TILE), NEG_INF, F32)
    stat_ref[2] = jnp.zeros((ROW_TILE, ROW_TILE), F32)
    stat_ref[3] = jnp.zeros((ROW_TILE, ROW_TILE), F32)


def _stats_update(stat_ref, sc):
    seen = jnp.where(sc == NEG_INF, POS_INF, sc)
    stat_ref[0] = jnp.minimum(stat_ref[0], _fold(seen, jnp.minimum))
    stat_ref[1] = jnp.maximum(stat_ref[1], _fold(sc, jnp.maximum))
    stat_ref[2] += _fold(jnp.where(sc >= 0.0, 1.0, 0.0))
    stat_ref[3] += _fold(jnp.where(sc > 0.0, 1.0, 0.0))


def _select_topk(sc_ref, stat_ref, nch, n_vis, topk):
    kf = float(topk)
    wide = lambda v: jnp.broadcast_to(v, (ROW_TILE, KC))
    rmin = jnp.min(stat_ref[0], axis=1, keepdims=True)
    rmax = jnp.max(stat_ref[1], axis=1, keepdims=True)
    cnt_ge0 = jnp.sum(stat_ref[2], axis=1, keepdims=True)
    cnt_gt0 = jnp.sum(stat_ref[3], axis=1, keepdims=True)
    needs = n_vis > kf

    def any_row(flag):
        return jnp.max(jnp.where(flag, 1.0, 0.0))

    above = cnt_gt0 >= kf
    below = cnt_ge0 < kf
    zero_tied = needs & jnp.logical_not(above) & jnp.logical_not(below)

    def smallest_positive():
        def body(c, acc):
            sc = sc_ref[c]
            return jnp.minimum(acc, _fold(jnp.where(sc > 0.0, sc, POS_INF), jnp.minimum))

        acc = lax.fori_loop(0, nch, body, jnp.full((ROW_TILE, ROW_TILE), POS_INF, F32))
        return jnp.min(acc, axis=1, keepdims=True)

    min_pos = lax.cond(any_row(zero_tied) > 0.0, smallest_positive,
                       lambda: jnp.full((ROW_TILE, 1), POS_INF, F32))

    lo0 = jnp.where(below, rmin, 0.0)
    cnt_lo0 = jnp.where(below, n_vis, cnt_ge0)
    hi_top = rmax + (0.5 * (jnp.abs(rmax) + jnp.abs(rmin)) + 1e-30)
    hi0 = jnp.where(above, hi_top, jnp.where(zero_tied, min_pos, 0.0))
    cnt_hi0 = jnp.where(above, 0.0, jnp.where(zero_tied, cnt_gt0, cnt_ge0))
    search = needs & jnp.logical_not(zero_tied)

    def searching(lo, hi, cnt_lo):
        mid = lo + 0.5 * (hi - lo)
        return mid, search & (cnt_lo != kf) & (mid > lo) & (mid < hi)

    def cond(st):
        return (st[0] < MAX_BISECTIONS) & (st[1] > 0.0)

    def body(st):
        it, _, lo, hi, cnt_lo, cnt_hi = st
        mid, open_ = searching(lo, hi, cnt_lo)
        mid_w = wide(mid)
        cnt = _count_rows(sc_ref, nch, lambda sc, c: sc >= mid_w)
        up = open_ & (cnt >= kf)
        dn = open_ & (cnt < kf)
        lo = jnp.where(up, mid, lo)
        cnt_lo = jnp.where(up, cnt, cnt_lo)
        hi = jnp.where(dn, mid, hi)
        cnt_hi = jnp.where(dn, cnt, cnt_hi)
        _, still = searching(lo, hi, cnt_lo)
        return it + 1, any_row(still), lo, hi, cnt_lo, cnt_hi

    _, open0 = searching(lo0, hi0, cnt_lo0)
    _, _, lo, hi, cnt_lo, cnt_hi = lax.while_loop(
        cond, body, (jnp.int32(0), any_row(open0), lo0, hi0, cnt_lo0, cnt_hi0))
    thr = jnp.where(needs, lo, rmin)

    surplus = needs & (cnt_lo > kf)

    @pl.when(any_row(surplus) > 0.0)
    def _():
        lo_w, hi_w, surplus_w = wide(lo), wide(hi), wide(surplus)
        lane = lax.broadcasted_iota(jnp.int32, (ROW_TILE, KC), 1)
        need = kf - cnt_hi

        def tied_before(bound):
            bound_w = wide(bound)
            return _count_rows(
                sc_ref, nch, lambda sc, c: (sc >= lo_w) & (sc < hi_w) & (lane + c * KC < bound_w))

        last_w = wide(_last_kept_position(tied_before, need, sc_ref.shape[0] * KC))

        def demote(c, carry):
            sc = sc_ref[c]
            drop = surplus_w & (sc >= lo_w) & (sc < hi_w) & (lane + c * KC > last_w)
            sc_ref[c] = jnp.where(drop, NEG_INF, sc)
            return carry

        lax.fori_loop(0, nch, demote, 0)

    return thr


def _prompt_attn_kernel(qi_ref, kiwi_ref, q_ref, ki2_ref, k_ref, v_ref, o_ref,
                        sc_ref, stat_ref, m_ref, l_ref, acc_ref, p_ref, *, topk):
    i = pl.program_id(0)
    nch = (i * ROW_TILE) // KC + 1
    lane = lax.broadcasted_iota(jnp.int32, (ROW_TILE, ROW_TILE), 1)
    lo_half = lane < HEAD_DIM
    qpos = i * ROW_TILE + lax.broadcasted_iota(jnp.int32, (ROW_TILE, KC), 0)
    kiota = lax.broadcasted_iota(jnp.int32, (ROW_TILE, KC), 1)

    def head_lanes(ref, h):
        p = h // 2
        blk = ref[:, p * ROW_TILE:(p + 1) * ROW_TILE]
        keep = lo_half if h % 2 == 0 else jnp.logical_not(lo_half)
        return jnp.where(keep, blk, jnp.zeros_like(blk))

    qis = [head_lanes(qi_ref, h) for h in range(N_HEADS)]
    wi = kiwi_ref[...]
    wis = [jnp.broadcast_to(wi[:, D_IDX + h:D_IDX + h + 1], (ROW_TILE, ROW_TILE)) for h in range(N_HEADS)]

    def score_body(c, carry):
        k0 = pl.multiple_of(c * KC, KC)
        kic = ki2_ref[pl.ds(k0, KC), :]
        parts = [jnp.zeros((ROW_TILE, ROW_TILE), F32)] * (KC // ROW_TILE)
        for h in range(N_HEADS):
            s = jnp.maximum(_nt_dot(qis[h], kic), 0.0)
            parts = [acc + tile * wis[h] for acc, tile in zip(parts, _lane_tiles(s))]
        sc = jnp.where(kiota + k0 <= qpos, jnp.concatenate(parts, axis=1), NEG_INF)
        sc_ref[c] = sc
        _stats_update(stat_ref, sc)
        return carry

    _stats_init(stat_ref)
    lax.fori_loop(0, nch, score_body, 0)

    n_vis = (qpos[:, 0:1] + 1).astype(F32)
    thr_w = jnp.broadcast_to(_select_topk(sc_ref, stat_ref, nch, n_vis, topk), (ROW_TILE, KC))

    qs = [head_lanes(q_ref, h) for h in range(N_HEADS)]
    m_ref[...] = jnp.full(m_ref.shape, NEG_BIG, F32)

    def masked_scores(c):
        k0 = pl.multiple_of(c * KC, KC)
        bias = jnp.where(sc_ref[c] >= thr_w, 0.0, NEG_BIG)
        return k0, bias

    def max_body(c, carry):
        k0, bias = masked_scores(c)
        for p in range(N_HEADS // 2):
            kp = k_ref[pl.ds(k0, KC), p * ROW_TILE:(p + 1) * ROW_TILE]
            for h in (2 * p, 2 * p + 1):
                s = _nt_dot(qs[h], kp) + bias
                m_ref[h] = jnp.maximum(m_ref[h], _fold(s, jnp.maximum))
        return carry

    lax.fori_loop(0, nch, max_body, 0)
    for h in range(N_HEADS):
        m_ref[h] = jnp.broadcast_to(jnp.max(m_ref[h], axis=1, keepdims=True), (ROW_TILE, ROW_TILE))

    l_ref[...] = jnp.zeros(l_ref.shape, F32)
    acc_ref[...] = jnp.zeros(acc_ref.shape, F32)

    p_ref[1] = jnp.zeros(p_ref.shape[1:], BF16)

    def weighted_values(c, slot):
        k0 = pl.multiple_of(c * KC, KC)
        for p in range(N_HEADS // 2):
            vp = v_ref[pl.ds(k0, KC), p * ROW_TILE:(p + 1) * ROW_TILE]
            for h in (2 * p, 2 * p + 1):
                acc_ref[h] += _dot(p_ref[slot, h], vp)

    def pv_body(c, carry):
        slot = c % 2
        k0, bias = masked_scores(c)
        for p in range(N_HEADS // 2):
            kp = k_ref[pl.ds(k0, KC), p * ROW_TILE:(p + 1) * ROW_TILE]
            for h in (2 * p, 2 * p + 1):
                s = _nt_dot(qs[h], kp) + bias
                m = m_ref[h]
                es = [jnp.exp(tile - m) for tile in _lane_tiles(s)]
                l_ref[h] += functools.reduce(jnp.add, es)
                p_ref[slot, h] = jnp.concatenate(es, axis=1).astype(BF16)
        weighted_values(jnp.maximum(c - 1, 0), 1 - slot)
        return carry

    lax.fori_loop(0, nch, pv_body, 0)
    weighted_values(nch - 1, (nch - 1) % 2)

    for p in range(N_HEADS // 2):
        even = acc_ref[2 * p] / jnp.sum(l_ref[2 * p], axis=1, keepdims=True)
        odd = acc_ref[2 * p + 1] / jnp.sum(l_ref[2 * p + 1], axis=1, keepdims=True)
        o_ref[:, p * ROW_TILE:(p + 1) * ROW_TILE] = jnp.where(lo_half, even, odd)


def prompt_attention(qi_bf, kiwi, q_bf, ki2_bf, k_bf, v_bf, *, n_blocks, topk):
    rows = k_bf.shape[0]
    blk = lambda w: pl.BlockSpec((ROW_TILE, w), lambda i: (i, 0))
    full = lambda a: pl.BlockSpec(a.shape, lambda i: (0, 0))
    head_tiles = pltpu.VMEM((N_HEADS, ROW_TILE, ROW_TILE), F32)
    return pl.pallas_call(
        functools.partial(_prompt_attn_kernel, topk=topk),
        grid=(n_blocks,),
        in_specs=[blk(ATT_W), blk(128), blk(ATT_W), full(ki2_bf), full(k_bf), full(v_bf)],
        out_specs=blk(ATT_W),
        out_shape=jax.ShapeDtypeStruct((n_blocks * ROW_TILE, ATT_W), F32),
        scratch_shapes=[pltpu.VMEM((rows // KC, ROW_TILE, KC), F32), pltpu.VMEM((4, ROW_TILE, ROW_TILE), F32),
                        head_tiles, head_tiles, head_tiles, pltpu.VMEM((2, N_HEADS, ROW_TILE, KC), BF16)],
        compiler_params=pltpu.CompilerParams(
            dimension_semantics=("arbitrary",), vmem_limit_bytes=VMEM_LIMIT),
        name="prompt_attn",
    )(qi_bf, kiwi, q_bf, ki2_bf, k_bf, v_bf)


HEAD_ROWS = 16
PAGE_GROUP = 8
PAGE_SLOTS = 8


def _sample_index_kernel(pt_ref, qi_ref, wi_ref, kin_ref, cik_ref, sc_out, thr_out,
                         kibuf, sc_ref, stat_ref, sem, *, topk, n_pages):
    b = pl.program_id(0)

    def page_copy(pg):
        return pltpu.make_async_copy(cik_ref.at[pt_ref[b * n_pages + pg]], kibuf.at[pg], sem.at[0])

    def start(pg, carry):
        page_copy(pg).start()
        return carry

    def wait(pg, carry):
        page_copy(pg).wait()
        return carry

    lax.fori_loop(0, n_pages, start, 0)
    lax.fori_loop(0, n_pages, wait, 0)

    qi = qi_ref[0]
    wi = wi_ref[0]
    pages_per_chunk = KC // PAGE_SIZE
    for c in range(n_pages // pages_per_chunk):
        parts = []
        for t in range(pages_per_chunk):
            kit = kibuf[c * pages_per_chunk + t].astype(BF16)
            s = jnp.maximum(_dot(qi, kit), 0.0) * wi
            parts.append(jnp.sum(s, axis=0, keepdims=True))
        sc_ref[c, pl.ds(b, 1), :] = jnp.concatenate(parts, axis=1)

    kin = kin_ref[0].astype(BF16).astype(F32)
    s_new = jnp.sum(qi.astype(F32) * kin, axis=1, keepdims=True)
    sc_new = jnp.sum(jnp.maximum(s_new, 0.0) * wi, axis=0, keepdims=True)
    lane = lax.broadcasted_iota(jnp.int32, (1, KC), 1)
    new_chunk = n_pages // pages_per_chunk
    sc_ref[new_chunk, pl.ds(b, 1), :] = jnp.where(lane == 0, sc_new, NEG_INF)

    @pl.when(b == pl.num_programs(0) - 1)
    def _():
        nch = new_chunk + 1
        n_vis = jnp.full((ROW_TILE, 1), float(n_pages * PAGE_SIZE + 1), F32)
        _stats_init(stat_ref)
        for c in range(nch):
            _stats_update(stat_ref, sc_ref[c])
        thr = _select_topk(sc_ref, stat_ref, nch, n_vis, topk)
        thr_out[...] = jnp.broadcast_to(thr, (ROW_TILE, ROW_TILE))
        for c in range(nch):
            sc_out[:, c * KC:(c + 1) * KC] = sc_ref[c]


def sample_index(page_table, qi_s, wi_s, ki_new, cache_ik_t, *, topk):
    nb, n_pages = page_table.shape
    assert nb == ROW_TILE and cache_ik_t.shape[1:] == (D_IDX, PAGE_SIZE)
    nch = n_pages * PAGE_SIZE // KC + 1
    per_q = lambda shape: pl.BlockSpec((1,) + shape, lambda b, pt: (b, 0, 0))
    full = lambda shape: pl.BlockSpec(shape, lambda b, pt: (0, 0))
    grid_spec = pltpu.PrefetchScalarGridSpec(
        num_scalar_prefetch=1,
        grid=(nb,),
        in_specs=[per_q((HEAD_ROWS, D_IDX)), per_q((HEAD_ROWS, 1)), per_q((1, D_IDX)),
                  pl.BlockSpec(memory_space=pl.ANY)],
        out_specs=[full((nb, nch * KC)), full((nb, ROW_TILE))],
        scratch_shapes=[
            pltpu.VMEM((n_pages, D_IDX, PAGE_SIZE), F32),
            pltpu.VMEM((nch, ROW_TILE, KC), F32),
            pltpu.VMEM((4, ROW_TILE, ROW_TILE), F32),
            pltpu.SemaphoreType.DMA((1,)),
        ],
    )
    return pl.pallas_call(
        functools.partial(_sample_index_kernel, topk=topk, n_pages=n_pages),
        grid_spec=grid_spec,
        out_shape=[jax.ShapeDtypeStruct((nb, nch * KC), F32), jax.ShapeDtypeStruct((nb, ROW_TILE), F32)],
        compiler_params=pltpu.CompilerParams(
            dimension_semantics=("arbitrary",), vmem_limit_bytes=VMEM_LIMIT),
        name="sample_index",
    )(page_table.reshape(-1), qi_s, wi_s, ki_new, cache_ik_t)


def _sample_attn_kernel(pt_ref, bdq_ref, sc_ref, thr_ref, kn_ref, vn_ref, ck_ref, cv_ref, o_ref,
                        buf, s_ref, sem, *, n_pages):
    b = pl.program_id(0)
    past = n_pages * PAGE_SIZE
    units_per_cache = n_pages // PAGE_GROUP
    n_units = 2 * units_per_cache

    def unit_copies(q, t):
        cache = ck_ref if t < units_per_cache else cv_ref
        first = (t % units_per_cache) * PAGE_GROUP
        slot = t % PAGE_SLOTS
        return [pltpu.make_async_copy(cache.at[pt_ref[q * n_pages + first + j]], buf.at[slot, j], sem.at[slot])
                for j in range(PAGE_GROUP)]

    def start_unit(q, t):
        for cp in unit_copies(q, t):
            cp.start()

    ahead = PAGE_SLOTS - 1

    @pl.when(b == 0)
    def _():
        for t in range(ahead):
            start_unit(b, t)

    bdq = bdq_ref[0]
    out = jnp.zeros((HEAD_ROWS, ATT_W), F32)
    denom = None
    for t in range(n_units):
        for cp in unit_copies(b, t):
            cp.wait()
        if t + ahead < n_units:
            start_unit(b, t + ahead)
        else:
            @pl.when(b + 1 < pl.num_programs(0))
            def _(t=t):
                start_unit(b + 1, t + ahead - n_units)

        slot = t % PAGE_SLOTS
        for j in range(PAGE_GROUP):
            pg = (t % units_per_cache) * PAGE_GROUP + j
            lanes = slice(pg * PAGE_SIZE, (pg + 1) * PAGE_SIZE)
            page = buf[slot, j].reshape(ATT_W, PAGE_SIZE).astype(BF16)
            if t < units_per_cache:
                s_ref[:, lanes] = _dot(bdq, page)
            else:
                out = out + _nt_dot(s_ref[:, lanes].astype(BF16), page)

        if t == units_per_cache - 1:
            sc = sc_ref[0]
            thr = thr_ref[0][:, 0:1]
            sel = sc[:, :past] >= thr
            sel_new = sc[:, past:past + 1] >= thr
            s = s_ref[...] + jnp.where(sel, 0.0, NEG_BIG)
            kn = kn_ref[0].astype(BF16).astype(F32)
            s_new = jnp.sum(bdq.astype(F32) * kn, axis=1, keepdims=True)
            s_new = jnp.where(sel_new, s_new, NEG_BIG)
            m = jnp.maximum(jnp.max(s, axis=1, keepdims=True), s_new)
            pr = jnp.exp(s - m)
            p_new = jnp.where(sel_new, jnp.exp(s_new - m), 0.0)
            denom = jnp.sum(pr, axis=1, keepdims=True) + p_new
            s_ref[...] = pr
            vn = vn_ref[0].astype(BF16).astype(F32)
            out = p_new.astype(BF16).astype(F32) * vn

    out = out / denom
    head_of_lane = lax.broadcasted_iota(jnp.int32, (HEAD_ROWS, ATT_W), 1) // HEAD_DIM
    head_of_row = lax.broadcasted_iota(jnp.int32, (HEAD_ROWS, ATT_W), 0)
    o_ref[0] = jnp.sum(jnp.where(head_of_lane == head_of_row, out, 0.0), axis=0, keepdims=True)


def sample_attention(page_table, bdq, sc_all, thr, k_new, v_new, cache_k_t, cache_v_t):
    nb, n_pages = page_table.shape
    assert cache_k_t.shape[1:] == (N_HEADS, HEAD_DIM, PAGE_SIZE)
    assert (2 * n_pages // PAGE_GROUP) % PAGE_SLOTS == 0
    past = n_pages * PAGE_SIZE
    per_q = lambda shape: pl.BlockSpec((1,) + shape, lambda b, pt: (b, 0, 0))
    any_spec = pl.BlockSpec(memory_space=pl.ANY)
    grid_spec = pltpu.PrefetchScalarGridSpec(
        num_scalar_prefetch=1,
        grid=(nb,),
        in_specs=[per_q((HEAD_ROWS, ATT_W)), per_q((1, sc_all.shape[-1])), per_q((1, ROW_TILE)),
                  per_q((1, ATT_W)), per_q((1, ATT_W)), any_spec, any_spec],
        out_specs=per_q((1, ATT_W)),
        scratch_shapes=[
            pltpu.VMEM((PAGE_SLOTS, PAGE_GROUP, N_HEADS, HEAD_DIM, PAGE_SIZE), F32),
            pltpu.VMEM((HEAD_ROWS, past), F32),
            pltpu.SemaphoreType.DMA((PAGE_SLOTS,)),
        ],
    )
    return pl.pallas_call(
        functools.partial(_sample_attn_kernel, n_pages=n_pages),
        grid_spec=grid_spec,
        out_shape=jax.ShapeDtypeStruct((nb, 1, ATT_W), F32),
        compiler_params=pltpu.CompilerParams(
            dimension_semantics=("arbitrary",), vmem_limit_bytes=VMEM_LIMIT),
        name="sample_attn",
    )(page_table.reshape(-1), bdq, sc_all.reshape(nb, 1, -1), thr.reshape(nb, 1, ROW_TILE),
      k_new, v_new, cache_k_t, cache_v_t)


def _merge_ln_kernel(uc_ref, att_ref, gs_ref, x1_ref, wb0_ref, wb1_ref, wo_ref, g_ref, b_ref, o_ref):
    conv_br = _dot(uc_ref[...].astype(BF16), wb0_ref[...])
    att_br = _dot(att_ref[...].astype(BF16), wb1_ref[...])
    m = gs_ref[:, :D_MODEL] * conv_br + gs_ref[:, D_MODEL:] * att_br
    h = _dot(m.astype(BF16), wo_ref[...])
    o_ref[...] = _layer_norm(ALPHA * x1_ref[...] + h, g_ref[...], b_ref[...])


def merge_ln(u_conv, att, gs, x1, wb0, wb1, wo, g, b, *, tm):
    rows, d = x1.shape
    row_spec = lambda w: pl.BlockSpec((tm, w), lambda i: (i, 0))
    full = lambda a: pl.BlockSpec(a.shape, lambda i: (0, 0))
    g2, b2 = g.reshape(1, d), b.reshape(1, d)
    return pl.pallas_call(
        _merge_ln_kernel,
        grid=(rows // tm,),
        in_specs=[row_spec(D_CONV), row_spec(ATT_W), row_spec(2 * d), row_spec(d),
                  full(wb0), full(wb1), full(wo), full(g2), full(b2)],
        out_specs=row_spec(d),
        out_shape=jax.ShapeDtypeStruct((rows, d), F32),
        compiler_params=pltpu.CompilerParams(
            dimension_semantics=("parallel",), vmem_limit_bytes=VMEM_LIMIT),
        name="merge_ln",
    )(u_conv, att, gs, x1, wb0, wb1, wo, g2, b2)


def kernel(x_prompt, x_sample, cache_k, cache_v, cache_idx_k, state_conv, page_table, meta_tokens,
           ffn1_w_in, ffn1_w_out, ln1_g, ln1_b, w_in, conv_w, conv_b, conv_ln_g, conv_ln_b, b_gate,
           w_branch, w_out, ln2_g, ln2_b, ffn2_w_in, ffn2_w_out, ln3_g, ln3_b):
    topk_prompt = min(TOPK, SEQ // 4)
    topk_sample = min(TOPK, (PAST_LEN + 1) // 4)

    x_all = jnp.concatenate([
        meta_tokens.astype(F32), x_prompt[0],
        jnp.zeros((LP - L_PROMPT, D_MODEL), F32), x_sample[:, 0],
        jnp.zeros((R_ALL - S_ROW1, D_MODEL), F32)], axis=0)
    wa = w_in[:, :C_KI].astype(BF16)
    w_ki = w_in[:, C_KI:C_WI]
    wkw = jnp.concatenate([w_in[:, C_KI:C_G], jnp.zeros((D_MODEL, 128 - D_IDX - N_HEADS), F32)],
                          axis=1).astype(BF16)
    wk2 = jnp.concatenate([w_ki, w_ki], axis=1).astype(BF16)
    wg = w_in[:, C_G:].astype(BF16)

    x1 = ffn_ln(x_all, ffn1_w_in.astype(BF16), ffn1_w_out.astype(BF16), ln1_g, ln1_b, tm=512, tf=D_FF // 2)
    u, q_bf, k, v, k_bf, v_bf, qi_bf, kiwi, ki2_bf, gs = in_proj(x1, wa, wkw, wk2, wg, b_gate, tm=256)
    sample_rows = lambda a: a[S_ROW0:S_ROW1]
    tail_rows = jnp.zeros((R_ALL - S_ROW1, ATT_W), F32)

    uc_prompt = conv_prompt(u, conv_w, conv_b, conv_ln_g, conv_ln_b, rows=LP, tm=384)
    uc_sample = conv_sample(jnp.transpose(state_conv, (1, 0, 2)), u, conv_w, conv_b, conv_ln_g, conv_ln_b,
                            row0=S_ROW0)
    u_conv = jnp.concatenate([uc_prompt, uc_sample, tail_rows], axis=0)

    att_prompt = prompt_attention(qi_bf, kiwi, q_bf, ki2_bf, k_bf, v_bf,
                                  n_blocks=LP // ROW_TILE, topk=topk_prompt)
    pad_heads = lambda a: jnp.pad(a, ((0, 0), (0, HEAD_ROWS - N_HEADS), (0, 0)))
    kiwi_s, k_s, v_s, u_s = sample_rows(kiwi), sample_rows(k), sample_rows(v), sample_rows(u)
    qi_s = pad_heads(sample_rows(qi_bf).reshape(DEC_BATCH, N_HEADS, D_IDX))
    wi_s = pad_heads(kiwi_s[:, D_IDX:D_IDX + N_HEADS].reshape(DEC_BATCH, N_HEADS, 1))
    head_of_lane = jnp.arange(ATT_W) // HEAD_DIM
    bdq = jnp.where(head_of_lane[None, None, :] == jnp.arange(HEAD_ROWS)[None, :, None],
                    sample_rows(q_bf)[:, None, :], jnp.zeros((), BF16))
    cache_ik_t = jnp.transpose(cache_idx_k, (0, 2, 1))
    cache_k_t = jnp.transpose(cache_k, (0, 2, 3, 1))
    cache_v_t = jnp.transpose(cache_v, (0, 2, 3, 1))
    sc_all, thr = sample_index(page_table, qi_s, wi_s, kiwi_s[:, :D_IDX].reshape(DEC_BATCH, 1, D_IDX),
                               cache_ik_t, topk=topk_sample)
    att_sample = sample_attention(
        page_table, bdq, sc_all, thr,
        k_s.reshape(DEC_BATCH, 1, ATT_W), v_s.reshape(DEC_BATCH, 1, ATT_W), cache_k_t, cache_v_t)
    att = jnp.concatenate([att_prompt, att_sample[:, 0], tail_rows], axis=0)

    x2 = merge_ln(u_conv, att, gs, x1, w_branch[0].astype(BF16), w_branch[1].astype(BF16),
                  w_out.astype(BF16), ln2_g, ln2_b, tm=512)
    y = ffn_ln(x2, ffn2_w_in.astype(BF16), ffn2_w_out.astype(BF16), ln3_g, ln3_b, tm=512, tf=D_FF // 2)

    hd = (N_HEADS, HEAD_DIM)
    y_prompt = y[N_META:L_PROMPT][None]
    y_sample = sample_rows(y)[:, None]
    k_prompt = k[:L_PROMPT].reshape((1, L_PROMPT) + hd)
    v_prompt = v[:L_PROMPT].reshape((1, L_PROMPT) + hd)
    idxk_prompt = kiwi[:L_PROMPT, :D_IDX][None]
    conv_prompt_state = u[L_PROMPT - (CONV_W - 1):L_PROMPT][None]
    k_sample = k_s.reshape((DEC_BATCH, 1) + hd)
    v_sample = v_s.reshape((DEC_BATCH, 1) + hd)
    idxk_sample = kiwi_s[:, :D_IDX][:, None]
    conv_sample_state = jnp.concatenate([state_conv[:, 1:].astype(F32), u_s[:, None]], axis=1)
    return (y_prompt, y_sample, k_prompt, v_prompt, idxk_prompt, conv_prompt_state,
            k_sample, v_sample, idxk_sample, conv_sample_state)
```

```python
import functools

import jax
import jax.numpy as jnp
from jax import lax
from jax.experimental import pallas as pl
from jax.experimental.pallas import tpu as pltpu

D_MODEL = 1024
SEQ = 16384
N_META = 16
L_PROMPT = SEQ + N_META
DEC_BATCH = 128
PAST_LEN = 8192
PAGE_SIZE = 128
N_PAGES = PAST_LEN // PAGE_SIZE
D_CONV = 512
CONV_W = 31
N_HEADS = 8
HEAD_DIM = 64
ATT_W = N_HEADS * HEAD_DIM
D_IDX = 64
TOPK = 256
D_FF = 2816
ALPHA = 2.0 ** 0.25
LN_EPS = 1e-5

ROW_TILE = 128
LP = 16512
S_ROW0 = LP
S_ROW1 = S_ROW0 + DEC_BATCH
R_ALL = 17408

C_Q = 2 * D_CONV
C_K = C_Q + ATT_W
C_V = C_K + ATT_W
C_QI = C_V + ATT_W
C_KI = C_QI + N_HEADS * D_IDX
C_WI = C_KI + D_IDX
C_G = C_WI + N_HEADS

NEG_BIG = -1e30
VMEM_LIMIT = 60 * 1024 * 1024

BF16 = jnp.bfloat16
F32 = jnp.float32


def _nt_dot(a, b):
    return lax.dot_general(a, b, (((1,), (1,)), ((), ())), preferred_element_type=F32)


def _dot(a, b):
    return jnp.dot(a, b, preferred_element_type=F32)


def _sigmoid(x):
    return 1.0 / (1.0 + jnp.exp(-x))


def _layer_norm(x, g, b):
    mu = jnp.mean(x, axis=-1, keepdims=True)
    xc = x - mu
    var = jnp.mean(xc * xc, axis=-1, keepdims=True)
    return xc * lax.rsqrt(var + LN_EPS) * g + b


def _ffn_ln_kernel(x_ref, wg_ref, wu_ref, wo_ref, g_ref, b_ref, o_ref, acc_ref):
    j = pl.program_id(1)
    xb = x_ref[...].astype(BF16)
    gate = _dot(xb, wg_ref[...])
    up = _dot(xb, wu_ref[...])
    h = (gate * _sigmoid(gate)) * up
    part = _dot(h.astype(BF16), wo_ref[...])

    @pl.when(j == 0)
    def _():
        acc_ref[...] = part

    @pl.when(j > 0)
    def _():
        acc_ref[...] += part

    @pl.when(j == pl.num_programs(1) - 1)
    def _():
        y = ALPHA * x_ref[...] + 0.5 * acc_ref[...]
        o_ref[...] = _layer_norm(y, g_ref[...], b_ref[...])


def ffn_ln(x, w_in_bf, w_out_bf, g, b, *, tm, tf):
    rows, d = x.shape
    d_ff = w_out_bf.shape[0]
    nf = d_ff // tf
    return pl.pallas_call(
        _ffn_ln_kernel,
        grid=(rows // tm, nf),
        in_specs=[
            pl.BlockSpec((tm, d), lambda i, j: (i, 0)),
            pl.BlockSpec((d, tf), lambda i, j: (0, j)),
            pl.BlockSpec((d, tf), lambda i, j, nf=nf: (0, j + nf)),
            pl.BlockSpec((tf, d), lambda i, j: (j, 0)),
            pl.BlockSpec((1, d), lambda i, j: (0, 0)),
            pl.BlockSpec((1, d), lambda i, j: (0, 0)),
        ],
        out_specs=pl.BlockSpec((tm, d), lambda i, j: (i, 0)),
        out_shape=jax.ShapeDtypeStruct((rows, d), F32),
        scratch_shapes=[pltpu.VMEM((tm, d), F32)],
        compiler_params=pltpu.CompilerParams(
            dimension_semantics=("parallel", "arbitrary"), vmem_limit_bytes=VMEM_LIMIT),
        name="ffn_ln",
    )(x, w_in_bf, w_in_bf, w_out_bf, g.reshape(1, d), b.reshape(1, d))


def _in_proj_kernel(x_ref, wa_ref, wkw_ref, wk2_ref, wg_ref, bg_ref,
                    u_ref, qb_ref, k_ref, v_ref, kb_ref, vb_ref, qib_ref, kiwi_ref, ki2_ref, gs_ref):
    xb = x_ref[...].astype(BF16)
    glu = _dot(xb, wa_ref[:, 0:C_Q])
    u_ref[...] = glu[:, :D_CONV] * _sigmoid(glu[:, D_CONV:])
    q = _dot(xb, wa_ref[:, C_Q:C_K])
    qb_ref[...] = (q * (HEAD_DIM ** -0.5)).astype(BF16)
    k = _dot(xb, wa_ref[:, C_K:C_V])
    k_ref[...] = k
    kb_ref[...] = k.astype(BF16)
    v = _dot(xb, wa_ref[:, C_V:C_QI])
    v_ref[...] = v
    vb_ref[...] = v.astype(BF16)
    qib_ref[...] = _dot(xb, wa_ref[:, C_QI:C_KI]).astype(BF16)
    kiwi_ref[...] = _dot(xb, wkw_ref[...])
    ki2_ref[...] = _dot(xb, wk2_ref[...]).astype(BF16)
    gs_ref[...] = _sigmoid(_dot(xb, wg_ref[...]) + bg_ref[...])


def in_proj(x1, wa, wkw, wk2, wg, b_gate, *, tm):
    rows, d = x1.shape
    row_spec = lambda w: pl.BlockSpec((tm, w), lambda i: (i, 0))
    full = lambda a: pl.BlockSpec(a.shape, lambda i: (0, 0))
    bg = b_gate.reshape(1, -1)
    outs = [
        (D_CONV, F32),
        (ATT_W, BF16),
        (ATT_W, F32),
        (ATT_W, F32),
        (ATT_W, BF16),
        (ATT_W, BF16),
        (ATT_W, BF16),
        (128, F32),
        (128, BF16),
        (2 * D_MODEL, F32),
    ]
    return pl.pallas_call(
        _in_proj_kernel,
        grid=(rows // tm,),
        in_specs=[row_spec(d), full(wa), full(wkw), full(wk2), full(wg), full(bg)],
        out_specs=[row_spec(w) for w, _ in outs],
        out_shape=[jax.ShapeDtypeStruct((rows, w), dt) for w, dt in outs],
        compiler_params=pltpu.CompilerParams(
            dimension_semantics=("parallel",), vmem_limit_bytes=VMEM_LIMIT),
        name="in_proj",
    )(x1, wa, wkw, wk2, wg, bg)


CONV_HALO = 32
CONV_SUB = 64


def _conv_post(c, cb_ref, g_ref, b_ref):
    y = _layer_norm(c + cb_ref[...], g_ref[...], b_ref[...])
    return y * _sigmoid(y)


def _conv_prompt_kernel(halo_ref, cur_ref, w_ref, cb_ref, g_ref, b_ref, o_ref, ext_ref, *, tm):
    i = pl.program_id(0)

    @pl.when(i == 0)
    def _():
        ext_ref[0:CONV_HALO, :] = jnp.zeros((CONV_HALO, D_CONV), F32)

    @pl.when(i > 0)
    def _():
        ext_ref[0:CONV_HALO, :] = halo_ref[...]

    ext_ref[CONV_HALO:, :] = cur_ref[...]
    off = CONV_HALO - (CONV_W - 1)
    for r0 in range(0, tm, CONV_SUB):
        acc = jnp.zeros((CONV_SUB, D_CONV), F32)
        for j in range(CONV_W):
            acc = acc + ext_ref[r0 + off + j:r0 + off + j + CONV_SUB, :] * w_ref[j:j + 1, :]
        o_ref[r0:r0 + CONV_SUB, :] = _conv_post(acc, cb_ref, g_ref, b_ref)


def conv_prompt(u, conv_w, conv_b, g, b, *, rows, tm):
    per = tm // CONV_HALO
    vec = lambda a: a.reshape(1, D_CONV)
    full = lambda shape: pl.BlockSpec(shape, lambda i: (0, 0))
    return pl.pallas_call(
        functools.partial(_conv_prompt_kernel, tm=tm),
        grid=(rows // tm,),
        in_specs=[
            pl.BlockSpec((CONV_HALO, D_CONV), lambda i: (jnp.maximum(i * per - 1, 0), 0)),
            pl.BlockSpec((tm, D_CONV), lambda i: (i, 0)),
            full((CONV_W, D_CONV)), full((1, D_CONV)), full((1, D_CONV)), full((1, D_CONV)),
        ],
        out_specs=pl.BlockSpec((tm, D_CONV), lambda i: (i, 0)),
        out_shape=jax.ShapeDtypeStruct((rows, D_CONV), F32),
        scratch_shapes=[pltpu.VMEM((CONV_HALO + tm, D_CONV), F32)],
        compiler_params=pltpu.CompilerParams(dimension_semantics=("arbitrary",)),
        name="conv_prompt",
    )(u, u, conv_w, vec(conv_b), vec(g), vec(b))


def _conv_sample_kernel(state_ref, us_ref, w_ref, cb_ref, g_ref, b_ref, o_ref):
    acc = us_ref[...] * w_ref[CONV_W - 1:CONV_W, :]
    for j in range(CONV_W - 1):
        acc = acc + state_ref[j] * w_ref[j:j + 1, :]
    o_ref[...] = _conv_post(acc, cb_ref, g_ref, b_ref)


def conv_sample(state_t, u, conv_w, conv_b, g, b, *, row0):
    nb = state_t.shape[1]
    vec = lambda a: a.reshape(1, D_CONV)
    full = lambda shape: pl.BlockSpec(shape, lambda i: (0,) * len(shape))
    assert row0 % nb == 0
    return pl.pallas_call(
        _conv_sample_kernel,
        grid=(1,),
        in_specs=[
            full(state_t.shape),
            pl.BlockSpec((nb, D_CONV), lambda i: (row0 // nb, 0)),
            full((CONV_W, D_CONV)), full((1, D_CONV)), full((1, D_CONV)), full((1, D_CONV)),
        ],
        out_specs=full((nb, D_CONV)),
        out_shape=jax.ShapeDtypeStruct((nb, D_CONV), F32),
        name="conv_sample",
    )(state_t, u, conv_w, vec(conv_b), vec(g), vec(b))


KC = 8 * ROW_TILE
KSUB = 2 * ROW_TILE
MAX_BISECTIONS = 400
NEG_INF = float("-inf")
POS_INF = float("inf")


def _lane_tiles(x):
    return [x[:, t * ROW_TILE:(t + 1) * ROW_TILE] for t in range(x.shape[1] // ROW_TILE)]


def _fold(x, op=jnp.add):
    return functools.reduce(op, _lane_tiles(x))


def _count_rows(sc_ref, nch, pred):
    def body(c, acc):
        hits = [jnp.where(pred(tile, c * KC + t * ROW_TILE), 1.0, 0.0)
                for t, tile in enumerate(_lane_tiles(sc_ref[c]))]
        return acc + functools.reduce(jnp.add, hits)

    acc = lax.fori_loop(0, nch, body, jnp.zeros((ROW_TILE, ROW_TILE), F32))
    return jnp.sum(acc, axis=1, keepdims=True)


def _last_kept_position(tied_before, need, n_pos):
    nbits = (n_pos - 1).bit_length()

    def bit_body(b, j):
        cand = j | lax.shift_left(jnp.int32(1), nbits - 1 - b)
        return jnp.where(tied_before(cand) < need, cand, j)

    return lax.fori_loop(0, nbits, bit_body, jnp.zeros((ROW_TILE, 1), jnp.int32))


def _stats_init(stat_ref):
    stat_ref[0] = jnp.full((ROW_TILE, ROW_TILE), POS_INF, F32)
    stat_ref[1] = jnp.full((ROW_TILE, ROW_TILE), NEG_INF, F32)
    stat_ref[2] = jnp.zeros((ROW_TILE, ROW_TILE), F32)
    stat_ref[3] = jnp.zeros((ROW_TILE, ROW_TILE), F32)


def _stats_update(stat_ref, sc):
    seen = jnp.where(sc == NEG_INF, POS_INF, sc)
    stat_ref[0] = jnp.minimum(stat_ref[0], _fold(seen, jnp.minimum))
    stat_ref[1] = jnp.maximum(stat_ref[1], _fold(sc, jnp.maximum))
    stat_ref[2] += _fold(jnp.where(sc >= 0.0, 1.0, 0.0))
    stat_ref[3] += _fold(jnp.where(sc > 0.0, 1.0, 0.0))


def _select_topk(sc_ref, stat_ref, nch, n_vis, topk):
    kf = float(topk)
    wide = lambda v: jnp.broadcast_to(v, (ROW_TILE, ROW_TILE))
    rmin = jnp.min(stat_ref[0], axis=1, keepdims=True)
    rmax = jnp.max(stat_ref[1], axis=1, keepdims=True)
    cnt_ge0 = jnp.sum(stat_ref[2], axis=1, keepdims=True)
    cnt_gt0 = jnp.sum(stat_ref[3], axis=1, keepdims=True)
    needs = n_vis > kf

    def any_row(flag):
        return jnp.max(jnp.where(flag, 1.0, 0.0))

    above = cnt_gt0 >= kf
    below = cnt_ge0 < kf
    zero_tied = needs & jnp.logical_not(above) & jnp.logical_not(below)

    def smallest_positive():
        def body(c, acc):
            sc = sc_ref[c]
            return jnp.minimum(acc, _fold(jnp.where(sc > 0.0, sc, POS_INF), jnp.minimum))

        acc = lax.fori_loop(0, nch, body, jnp.full((ROW_TILE, ROW_TILE), POS_INF, F32))
        return jnp.min(acc, axis=1, keepdims=True)

    min_pos = lax.cond(any_row(zero_tied) > 0.0, smallest_positive,
                       lambda: jnp.full((ROW_TILE, 1), POS_INF, F32))

    lo0 = jnp.where(below, rmin, 0.0)
    cnt_lo0 = jnp.where(below, n_vis, cnt_ge0)
    hi_top = rmax + (0.5 * (jnp.abs(rmax) + jnp.abs(rmin)) + 1e-30)
    hi0 = jnp.where(above, hi_top, jnp.where(zero_tied, min_pos, 0.0))
    cnt_hi0 = jnp.where(above, 0.0, jnp.where(zero_tied, cnt_gt0, cnt_ge0))
    search = jnp.where(needs & jnp.logical_not(zero_tied), 1.0, 0.0)

    def probe(lo, hi, cnt_lo):
        mid = lo + 0.5 * (hi - lo)
        return mid, jnp.where((cnt_lo != kf) & (mid > lo) & (mid < hi), search, 0.0)

    def cond(st):
        return (st[0] < MAX_BISECTIONS) & (st[1] > 0.0)

    def body(st):
        it, _, lo, hi, cnt_lo, cnt_hi, mid, open_f = st
        mid_w = wide(mid)
        cnt = _count_rows(sc_ref, nch, lambda tile, k0: tile >= mid_w)
        up = (open_f > 0.0) & (cnt >= kf)
        dn = (open_f > 0.0) & (cnt < kf)
        lo = jnp.where(up, mid, lo)
        cnt_lo = jnp.where(up, cnt, cnt_lo)
        hi = jnp.where(dn, mid, hi)
        cnt_hi = jnp.where(dn, cnt, cnt_hi)
        mid, open_f = probe(lo, hi, cnt_lo)
        return it + 1, jnp.max(open_f), lo, hi, cnt_lo, cnt_hi, mid, open_f

    mid0, open0 = probe(lo0, hi0, cnt_lo0)
    _, _, lo, hi, cnt_lo, cnt_hi, _, _ = lax.while_loop(
        cond, body, (jnp.int32(0), jnp.max(open0), lo0, hi0, cnt_lo0, cnt_hi0, mid0, open0))
    thr = jnp.where(needs, lo, rmin)

    surplus = needs & (cnt_lo > kf)

    @pl.when(any_row(surplus) > 0.0)
    def _():
        lo_w, hi_w, surplus_w = wide(lo), wide(hi), wide(surplus)
        lane = lax.broadcasted_iota(jnp.int32, (ROW_TILE, ROW_TILE), 1)
        need = kf - cnt_hi

        def tied_before(bound):
            bound_w = wide(bound)
            return _count_rows(
                sc_ref, nch, lambda tile, k0: (tile >= lo_w) & (tile < hi_w) & (lane + k0 < bound_w))

        last_w = wide(_last_kept_position(tied_before, need, sc_ref.shape[0] * KC))

        def demote(c, carry):
            kept = []
            for t, tile in enumerate(_lane_tiles(sc_ref[c])):
                pos = lane + (c * KC + t * ROW_TILE)
                drop = surplus_w & (tile >= lo_w) & (tile < hi_w) & (pos > last_w)
                kept.append(jnp.where(drop, NEG_INF, tile))
            sc_ref[c] = jnp.concatenate(kept, axis=1)
            return carry

        lax.fori_loop(0, nch, demote, 0)

    return thr


def _prompt_attn_kernel(qi_ref, kiwi_ref, q_ref, ki2_ref, k_ref, v_ref, o_ref,
                        sc_ref, stat_ref, m_ref, l_ref, acc_ref, p_ref, *, topk):
    i = pl.program_id(0)
    nch = (i * ROW_TILE) // KC + 1
    lane = lax.broadcasted_iota(jnp.int32, (ROW_TILE, ROW_TILE), 1)
    lo_half = lane < HEAD_DIM
    qpos = i * ROW_TILE + lax.broadcasted_iota(jnp.int32, (ROW_TILE, KSUB), 0)
    kiota = lax.broadcasted_iota(jnp.int32, (ROW_TILE, KSUB), 1)

    def head_lanes(ref, h):
        p = h // 2
        blk = ref[:, p * ROW_TILE:(p + 1) * ROW_TILE]
        keep = lo_half if h % 2 == 0 else jnp.logical_not(lo_half)
        return jnp.where(keep, blk, jnp.zeros_like(blk))

    qis = [head_lanes(qi_ref, h) for h in range(N_HEADS)]
    wi = kiwi_ref[...]
    wis = [jnp.broadcast_to(wi[:, D_IDX + h:D_IDX + h + 1], (ROW_TILE, ROW_TILE)) for h in range(N_HEADS)]

    def score_body(c, carry):
        for j in range(KC // KSUB):
            k0 = pl.multiple_of(c * KC, KC) + j * KSUB
            kic = ki2_ref[pl.ds(k0, KSUB), :]
            parts = [jnp.zeros((ROW_TILE, ROW_TILE), F32)] * (KSUB // ROW_TILE)
            for h in range(N_HEADS):
                s = jnp.maximum(_nt_dot(qis[h], kic), 0.0)
                parts = [acc + tile * wis[h] for acc, tile in zip(parts, _lane_tiles(s))]
            sc = jnp.where(kiota + k0 <= qpos, jnp.concatenate(parts, axis=1), NEG_INF)
            sc_ref[c, :, j * KSUB:(j + 1) * KSUB] = sc
            _stats_update(stat_ref, sc)
        return carry

    _stats_init(stat_ref)
    lax.fori_loop(0, nch, score_body, 0)

    n_vis = (qpos[:, 0:1] + 1).astype(F32)
    thr_w = jnp.broadcast_to(_select_topk(sc_ref, stat_ref, nch, n_vis, topk), (ROW_TILE, ROW_TILE))

    qs = [head_lanes(q_ref, h) for h in range(N_HEADS)]
    m_ref[...] = jnp.full(m_ref.shape, NEG_BIG, F32)

    def masked_scores(c):
        k0 = pl.multiple_of(c * KC, KC)
        bias = jnp.concatenate(
            [jnp.where(tile >= thr_w, 0.0, NEG_BIG) for tile in _lane_tiles(sc_ref[c])], axis=1)
        return k0, bias

    def max_body(c, carry):
        k0, bias = masked_scores(c)
        for p in range(N_HEADS // 2):
            kp = k_ref[pl.ds(k0, KC), p * ROW_TILE:(p + 1) * ROW_TILE]
            for h in (2 * p, 2 * p + 1):
                s = _nt_dot(qs[h], kp) + bias
                m_ref[h] = jnp.maximum(m_ref[h], _fold(s, jnp.maximum))
        return carry

    lax.fori_loop(0, nch, max_body, 0)
    for h in range(N_HEADS):
        m_ref[h] = jnp.broadcast_to(jnp.max(m_ref[h], axis=1, keepdims=True), (ROW_TILE, ROW_TILE))

    l_ref[...] = jnp.zeros(l_ref.shape, F32)
    acc_ref[...] = jnp.zeros(acc_ref.shape, F32)

    p_ref[1] = jnp.zeros(p_ref.shape[1:], BF16)

    def weighted_values(c, slot):
        k0 = pl.multiple_of(c * KC, KC)
        for p in range(N_HEADS // 2):
            vp = v_ref[pl.ds(k0, KC), p * ROW_TILE:(p + 1) * ROW_TILE]
            for h in (2 * p, 2 * p + 1):
                acc_ref[h] += _dot(p_ref[slot, h], vp)

    def pv_body(c, carry):
        slot = c % 2
        for j in range(KC // KSUB):
            k0 = pl.multiple_of(c * KC, KC) + j * KSUB
            cols = slice(j * KSUB, (j + 1) * KSUB)
            bias = jnp.concatenate(
                [jnp.where(tile >= thr_w, 0.0, NEG_BIG) for tile in _lane_tiles(sc_ref[c, :, cols])], axis=1)
            for p in range(N_HEADS // 2):
                kp = k_ref[pl.ds(k0, KSUB), p * ROW_TILE:(p + 1) * ROW_TILE]
                for h in (2 * p, 2 * p + 1):
                    s = _nt_dot(qs[h], kp) + bias
                    m = m_ref[h]
                    es = [jnp.exp(tile - m) for tile in _lane_tiles(s)]
                    l_ref[h] += functools.reduce(jnp.add, es)
                    p_ref[slot, h, :, cols] = jnp.concatenate(es, axis=1).astype(BF16)
        weighted_values(jnp.maximum(c - 1, 0), 1 - slot)
        return carry

    lax.fori_loop(0, nch, pv_body, 0)
    weighted_values(nch - 1, (nch - 1) % 2)

    for p in range(N_HEADS // 2):
        even = acc_ref[2 * p] / jnp.sum(l_ref[2 * p], axis=1, keepdims=True)
        odd = acc_ref[2 * p + 1] / jnp.sum(l_ref[2 * p + 1], axis=1, keepdims=True)
        o_ref[:, p * ROW_TILE:(p + 1) * ROW_TILE] = jnp.where(lo_half, even, odd)


def prompt_attention(qi_bf, kiwi, q_bf, ki2_bf, k_bf, v_bf, *, n_blocks, topk):
    rows = k_bf.shape[0]
    blk = lambda w: pl.BlockSpec((ROW_TILE, w), lambda i: (i, 0))
    full = lambda a: pl.BlockSpec(a.shape, lambda i: (0, 0))
    head_tiles = pltpu.VMEM((N_HEADS, ROW_TILE, ROW_TILE), F32)
    return pl.pallas_call(
        functools.partial(_prompt_attn_kernel, topk=topk),
        grid=(n_blocks,),
        in_specs=[blk(ATT_W), blk(128), blk(ATT_W), full(ki2_bf), full(k_bf), full(v_bf)],
        out_specs=blk(ATT_W),
        out_shape=jax.ShapeDtypeStruct((n_blocks * ROW_TILE, ATT_W), F32),
        scratch_shapes=[pltpu.VMEM((rows // KC, ROW_TILE, KC), F32), pltpu.VMEM((4, ROW_TILE, ROW_TILE), F32),
                        head_tiles, head_tiles, head_tiles, pltpu.VMEM((2, N_HEADS, ROW_TILE, KC), BF16)],
        compiler_params=pltpu.CompilerParams(
            dimension_semantics=("arbitrary",), vmem_limit_bytes=VMEM_LIMIT),
        name="prompt_attn",
    )(qi_bf, kiwi, q_bf, ki2_bf, k_bf, v_bf)


HEAD_ROWS = 16
PAGE_GROUP = 8
PAGE_SLOTS = 8


def _sample_index_kernel(pt_ref, qi_ref, wi_ref, kin_ref, cik_ref, sc_out, thr_out,
                         kibuf, sc_ref, stat_ref, sem, *, topk, n_pages):
    b = pl.program_id(0)

    def page_copy(pg):
        return pltpu.make_async_copy(cik_ref.at[pt_ref[b * n_pages + pg]], kibuf.at[pg], sem.at[0])

    def start(pg, carry):
        page_copy(pg).start()
        return carry

    def wait(pg, carry):
        page_copy(pg).wait()
        return carry

    lax.fori_loop(0, n_pages, start, 0)
    lax.fori_loop(0, n_pages, wait, 0)

    qi = qi_ref[0]
    wi = wi_ref[0]
    pages_per_chunk = KC // PAGE_SIZE
    for c in range(n_pages // pages_per_chunk):
        parts = []
        for t in range(pages_per_chunk):
            kit = kibuf[c * pages_per_chunk + t].astype(BF16)
            s = jnp.maximum(_dot(qi, kit), 0.0) * wi
            parts.append(jnp.sum(s, axis=0, keepdims=True))
        sc_ref[c, pl.ds(b, 1), :] = jnp.concatenate(parts, axis=1)

    kin = kin_ref[0].astype(BF16).astype(F32)
    s_new = jnp.sum(qi.astype(F32) * kin, axis=1, keepdims=True)
    sc_new = jnp.sum(jnp.maximum(s_new, 0.0) * wi, axis=0, keepdims=True)
    lane = lax.broadcasted_iota(jnp.int32, (1, KC), 1)
    new_chunk = n_pages // pages_per_chunk
    sc_ref[new_chunk, pl.ds(b, 1), :] = jnp.where(lane == 0, sc_new, NEG_INF)

    @pl.when(b == pl.num_programs(0) - 1)
    def _():
        nch = new_chunk + 1
        n_vis = jnp.full((ROW_TILE, 1), float(n_pages * PAGE_SIZE + 1), F32)
        _stats_init(stat_ref)
        for c in range(nch):
            _stats_update(stat_ref, sc_ref[c])
        thr = _select_topk(sc_ref, stat_ref, nch, n_vis, topk)
        thr_out[...] = jnp.broadcast_to(thr, (ROW_TILE, ROW_TILE))
        for c in range(nch):
            sc_out[:, c * KC:(c + 1) * KC] = sc_ref[c]


def sample_index(page_table, qi_s, wi_s, ki_new, cache_ik_t, *, topk):
    nb, n_pages = page_table.shape
    assert nb == ROW_TILE and cache_ik_t.shape[1:] == (D_IDX, PAGE_SIZE)
    nch = n_pages * PAGE_SIZE // KC + 1
    per_q = lambda shape: pl.BlockSpec((1,) + shape, lambda b, pt: (b, 0, 0))
    full = lambda shape: pl.BlockSpec(shape, lambda b, pt: (0, 0))
    grid_spec = pltpu.PrefetchScalarGridSpec(
        num_scalar_prefetch=1,
        grid=(nb,),
        in_specs=[per_q((HEAD_ROWS, D_IDX)), per_q((HEAD_ROWS, 1)), per_q((1, D_IDX)),
                  pl.BlockSpec(memory_space=pl.ANY)],
        out_specs=[full((nb, nch * KC)), full((nb, ROW_TILE))],
        scratch_shapes=[
            pltpu.VMEM((n_pages, D_IDX, PAGE_SIZE), F32),
            pltpu.VMEM((nch, ROW_TILE, KC), F32),
            pltpu.VMEM((4, ROW_TILE, ROW_TILE), F32),
            pltpu.SemaphoreType.DMA((1,)),
        ],
    )
    return pl.pallas_call(
        functools.partial(_sample_index_kernel, topk=topk, n_pages=n_pages),
        grid_spec=grid_spec,
        out_shape=[jax.ShapeDtypeStruct((nb, nch * KC), F32), jax.ShapeDtypeStruct((nb, ROW_TILE), F32)],
        compiler_params=pltpu.CompilerParams(
            dimension_semantics=("arbitrary",), vmem_limit_bytes=VMEM_LIMIT),
        name="sample_index",
    )(page_table.reshape(-1), qi_s, wi_s, ki_new, cache_ik_t)


def _sample_attn_kernel(pt_ref, bdq_ref, sc_ref, thr_ref, kn_ref, vn_ref, ck_ref, cv_ref, o_ref,
                        buf, s_ref, sem, *, n_pages):
    b = pl.program_id(0)
    past = n_pages * PAGE_SIZE
    units_per_cache = n_pages // PAGE_GROUP
    n_units = 2 * units_per_cache

    def unit_copies(q, t):
        cache = ck_ref if t < units_per_cache else cv_ref
        first = (t % units_per_cache) * PAGE_GROUP
        slot = t % PAGE_SLOTS
        return [pltpu.make_async_copy(cache.at[pt_ref[q * n_pages + first + j]], buf.at[slot, j], sem.at[slot])
                for j in range(PAGE_GROUP)]

    def start_unit(q, t):
        for cp in unit_copies(q, t):
            cp.start()

    ahead = PAGE_SLOTS - 1

    @pl.when(b == 0)
    def _():
        for t in range(ahead):
            start_unit(b, t)

    bdq = bdq_ref[0]
    out = jnp.zeros((HEAD_ROWS, ATT_W), F32)
    denom = None
    for t in range(n_units):
        for cp in unit_copies(b, t):
            cp.wait()
        if t + ahead < n_units:
            start_unit(b, t + ahead)
        else:
            @pl.when(b + 1 < pl.num_programs(0))
            def _(t=t):
                start_unit(b + 1, t + ahead - n_units)

        slot = t % PAGE_SLOTS
        for j in range(PAGE_GROUP):
            pg = (t % units_per_cache) * PAGE_GROUP + j
            lanes = slice(pg * PAGE_SIZE, (pg + 1) * PAGE_SIZE)
            page = buf[slot, j].reshape(ATT_W, PAGE_SIZE).astype(BF16)
            if t < units_per_cache:
                s_ref[:, lanes] = _dot(bdq, page)
            else:
                out = out + _nt_dot(s_ref[:, lanes].astype(BF16), page)

        if t == units_per_cache - 1:
            sc = sc_ref[0]
            thr = thr_ref[0][:, 0:1]
            sel = sc[:, :past] >= thr
            sel_new = sc[:, past:past + 1] >= thr
            s = s_ref[...] + jnp.where(sel, 0.0, NEG_BIG)
            kn = kn_ref[0].astype(BF16).astype(F32)
            s_new = jnp.sum(bdq.astype(F32) * kn, axis=1, keepdims=True)
            s_new = jnp.where(sel_new, s_new, NEG_BIG)
            m = jnp.maximum(jnp.max(s, axis=1, keepdims=True), s_new)
            pr = jnp.exp(s - m)
            p_new = jnp.where(sel_new, jnp.exp(s_new - m), 0.0)
            denom = jnp.sum(pr, axis=1, keepdims=True) + p_new
            s_ref[...] = pr
            vn = vn_ref[0].astype(BF16).astype(F32)
            out = p_new.astype(BF16).astype(F32) * vn

    out = out / denom
    head_of_lane = lax.broadcasted_iota(jnp.int32, (HEAD_ROWS, ATT_W), 1) // HEAD_DIM
    head_of_row = lax.broadcasted_iota(jnp.int32, (HEAD_ROWS, ATT_W), 0)
    o_ref[0] = jnp.sum(jnp.where(head_of_lane == head_of_row, out, 0.0), axis=0, keepdims=True)


def sample_attention(page_table, bdq, sc_all, thr, k_new, v_new, cache_k_t, cache_v_t):
    nb, n_pages = page_table.shape
    assert cache_k_t.shape[1:] == (N_HEADS, HEAD_DIM, PAGE_SIZE)
    assert (2 * n_pages // PAGE_GROUP) % PAGE_SLOTS == 0
    past = n_pages * PAGE_SIZE
    per_q = lambda shape: pl.BlockSpec((1,) + shape, lambda b, pt: (b, 0, 0))
    any_spec = pl.BlockSpec(memory_space=pl.ANY)
    grid_spec = pltpu.PrefetchScalarGridSpec(
        num_scalar_prefetch=1,
        grid=(nb,),
        in_specs=[per_q((HEAD_ROWS, ATT_W)), per_q((1, sc_all.shape[-1])), per_q((1, ROW_TILE)),
                  per_q((1, ATT_W)), per_q((1, ATT_W)), any_spec, any_spec],
        out_specs=per_q((1, ATT_W)),
        scratch_shapes=[
            pltpu.VMEM((PAGE_SLOTS, PAGE_GROUP, N_HEADS, HEAD_DIM, PAGE_SIZE), F32),
            pltpu.VMEM((HEAD_ROWS, past), F32),
            pltpu.SemaphoreType.DMA((PAGE_SLOTS,)),
        ],
    )
    return pl.pallas_call(
        functools.partial(_sample_attn_kernel, n_pages=n_pages),
        grid_spec=grid_spec,
        out_shape=jax.ShapeDtypeStruct((nb, 1, ATT_W), F32),
        compiler_params=pltpu.CompilerParams(
            dimension_semantics=("arbitrary",), vmem_limit_bytes=VMEM_LIMIT),
        name="sample_attn",
    )(page_table.reshape(-1), bdq, sc_all.reshape(nb, 1, -1), thr.reshape(nb, 1, ROW_TILE),
      k_new, v_new, cache_k_t, cache_v_t)


def _merge_ln_kernel(uc_ref, att_ref, gs_ref, x1_ref, wb0_ref, wb1_ref, wo_ref, g_ref, b_ref, o_ref):
    conv_br = _dot(uc_ref[...].astype(BF16), wb0_ref[...])
    att_br = _dot(att_ref[...].astype(BF16), wb1_ref[...])
    m = gs_ref[:, :D_MODEL] * conv_br + gs_ref[:, D_MODEL:] * att_br
    h = _dot(m.astype(BF16), wo_ref[...])
    o_ref[...] = _layer_norm(ALPHA * x1_ref[...] + h, g_ref[...], b_ref[...])


def merge_ln(u_conv, att, gs, x1, wb0, wb1, wo, g, b, *, tm):
    rows, d = x1.shape
    row_spec = lambda w: pl.BlockSpec((tm, w), lambda i: (i, 0))
    full = lambda a: pl.BlockSpec(a.shape, lambda i: (0, 0))
    g2, b2 = g.reshape(1, d), b.reshape(1, d)
    return pl.pallas_call(
        _merge_ln_kernel,
        grid=(rows // tm,),
        in_specs=[row_spec(D_CONV), row_spec(ATT_W), row_spec(2 * d), row_spec(d),
                  full(wb0), full(wb1), full(wo), full(g2), full(b2)],
        out_specs=row_spec(d),
        out_shape=jax.ShapeDtypeStruct((rows, d), F32),
        compiler_params=pltpu.CompilerParams(
            dimension_semantics=("parallel",), vmem_limit_bytes=VMEM_LIMIT),
        name="merge_ln",
    )(u_conv, att, gs, x1, wb0, wb1, wo, g2, b2)


def kernel(x_prompt, x_sample, cache_k, cache_v, cache_idx_k, state_conv, page_table, meta_tokens,
           ffn1_w_in, ffn1_w_out, ln1_g, ln1_b, w_in, conv_w, conv_b, conv_ln_g, conv_ln_b, b_gate,
           w_branch, w_out, ln2_g, ln2_b, ffn2_w_in, ffn2_w_out, ln3_g, ln3_b):
    topk_prompt = min(TOPK, SEQ // 4)
    topk_sample = min(TOPK, (PAST_LEN + 1) // 4)

    x_all = jnp.concatenate([
        meta_tokens.astype(F32), x_prompt[0],
        jnp.zeros((LP - L_PROMPT, D_MODEL), F32), x_sample[:, 0],
        jnp.zeros((R_ALL - S_ROW1, D_MODEL), F32)], axis=0)
    wa = w_in[:, :C_KI].astype(BF16)
    w_ki = w_in[:, C_KI:C_WI]
    wkw = jnp.concatenate([w_in[:, C_KI:C_G], jnp.zeros((D_MODEL, 128 - D_IDX - N_HEADS), F32)],
                          axis=1).astype(BF16)
    wk2 = jnp.concatenate([w_ki, w_ki], axis=1).astype(BF16)
    wg = w_in[:, C_G:].astype(BF16)

    x1 = ffn_ln(x_all, ffn1_w_in.astype(BF16), ffn1_w_out.astype(BF16), ln1_g, ln1_b, tm=512, tf=D_FF // 2)
    u, q_bf, k, v, k_bf, v_bf, qi_bf, kiwi, ki2_bf, gs = in_proj(x1, wa, wkw, wk2, wg, b_gate, tm=256)
    sample_rows = lambda a: a[S_ROW0:S_ROW1]
    tail_rows = jnp.zeros((R_ALL - S_ROW1, ATT_W), F32)

    uc_prompt = conv_prompt(u, conv_w, conv_b, conv_ln_g, conv_ln_b, rows=LP, tm=384)
    uc_sample = conv_sample(jnp.transpose(state_conv, (1, 0, 2)), u, conv_w, conv_b, conv_ln_g, conv_ln_b,
                            row0=S_ROW0)
    u_conv = jnp.concatenate([uc_prompt, uc_sample, tail_rows], axis=0)

    att_prompt = prompt_attention(qi_bf, kiwi, q_bf, ki2_bf, k_bf, v_bf,
                                  n_blocks=LP // ROW_TILE, topk=topk_prompt)
    pad_heads = lambda a: jnp.pad(a, ((0, 0), (0, HEAD_ROWS - N_HEADS), (0, 0)))
    kiwi_s, k_s, v_s, u_s = sample_rows(kiwi), sample_rows(k), sample_rows(v), sample_rows(u)
    qi_s = pad_heads(sample_rows(qi_bf).reshape(DEC_BATCH, N_HEADS, D_IDX))
    wi_s = pad_heads(kiwi_s[:, D_IDX:D_IDX + N_HEADS].reshape(DEC_BATCH, N_HEADS, 1))
    head_of_lane = jnp.arange(ATT_W) // HEAD_DIM
    bdq = jnp.where(head_of_lane[None, None, :] == jnp.arange(HEAD_ROWS)[None, :, None],
                    sample_rows(q_bf)[:, None, :], jnp.zeros((), BF16))
    cache_ik_t = jnp.transpose(cache_idx_k, (0, 2, 1))
    cache_k_t = jnp.transpose(cache_k, (0, 2, 3, 1))
    cache_v_t = jnp.transpose(cache_v, (0, 2, 3, 1))
    sc_all, thr = sample_index(page_table, qi_s, wi_s, kiwi_s[:, :D_IDX].reshape(DEC_BATCH, 1, D_IDX),
                               cache_ik_t, topk=topk_sample)
    att_sample = sample_attention(
        page_table, bdq, sc_all, thr,
        k_s.reshape(DEC_BATCH, 1, ATT_W), v_s.reshape(DEC_BATCH, 1, ATT_W), cache_k_t, cache_v_t)
    att = jnp.concatenate([att_prompt, att_sample[:, 0], tail_rows], axis=0)

    x2 = merge_ln(u_conv, att, gs, x1, w_branch[0].astype(BF16), w_branch[1].astype(BF16),
                  w_out.astype(BF16), ln2_g, ln2_b, tm=512)
    y = ffn_ln(x2, ffn2_w_in.astype(BF16), ffn2_w_out.astype(BF16), ln3_g, ln3_b, tm=512, tf=D_FF // 2)

    hd = (N_HEADS, HEAD_DIM)
    y_prompt = y[N_META:L_PROMPT][None]
    y_sample = sample_rows(y)[:, None]
    k_prompt = k[:L_PROMPT].reshape((1, L_PROMPT) + hd)
    v_prompt = v[:L_PROMPT].reshape((1, L_PROMPT) + hd)
    idxk_prompt = kiwi[:L_PROMPT, :D_IDX][None]
    conv_prompt_state = u[L_PROMPT - (CONV_W - 1):L_PROMPT][None]
    k_sample = k_s.reshape((DEC_BATCH, 1) + hd)
    v_sample = v_s.reshape((DEC_BATCH, 1) + hd)
    idxk_sample = kiwi_s[:, :D_IDX][:, None]
    conv_sample_state = jnp.concatenate([state_conv[:, 1:].astype(F32), u_s[:, None]], axis=1)
    return (y_prompt, y_sample, k_prompt, v_prompt, idxk_prompt, conv_prompt_state,
            k_sample, v_sample, idxk_sample, conv_sample_state)
```

```python
import functools
import math

import jax
import jax.numpy as jnp
from jax import lax
from jax.experimental import pallas as pl
from jax.experimental.pallas import tpu as pltpu

D_MODEL = 1024
SEQ = 16384
N_META = 16
L_PROMPT = SEQ + N_META
DEC_BATCH = 128
PAST_LEN = 8192
PAGE_SIZE = 128
N_PAGES = PAST_LEN // PAGE_SIZE
D_CONV = 512
CONV_W = 31
N_HEADS = 8
HEAD_DIM = 64
ATT_W = N_HEADS * HEAD_DIM
D_IDX = 64
TOPK = 256
D_FF = 2816
ALPHA = 2.0 ** 0.25
LN_EPS = 1e-5

ROW_TILE = 128
LP = 16512
S_ROW0 = LP
S_ROW1 = S_ROW0 + DEC_BATCH
R_ALL = 17408

C_Q = 2 * D_CONV
C_K = C_Q + ATT_W
C_V = C_K + ATT_W
C_QI = C_V + ATT_W
C_KI = C_QI + N_HEADS * D_IDX
C_WI = C_KI + D_IDX
C_G = C_WI + N_HEADS

NEG_BIG = -1e30
VMEM_LIMIT = 60 * 1024 * 1024

BF16 = jnp.bfloat16
F32 = jnp.float32


def _nt_dot(a, b):
    return lax.dot_general(a, b, (((1,), (1,)), ((), ())), preferred_element_type=F32)


def _dot(a, b):
    return jnp.dot(a, b, preferred_element_type=F32)


def _sigmoid(x):
    return 1.0 / (1.0 + jnp.exp(-x))


def _layer_norm(x, g, b):
    mu = jnp.mean(x, axis=-1, keepdims=True)
    xc = x - mu
    var = jnp.mean(xc * xc, axis=-1, keepdims=True)
    return xc * lax.rsqrt(var + LN_EPS) * g + b


def _ffn_ln_kernel(x_ref, wg_ref, wu_ref, wo_ref, g_ref, b_ref, o_ref, acc_ref):
    j = pl.program_id(1)
    xb = x_ref[...].astype(BF16)
    gate = _dot(xb, wg_ref[...])
    up = _dot(xb, wu_ref[...])
    h = (gate * _sigmoid(gate)) * up
    part = _dot(h.astype(BF16), wo_ref[...])

    @pl.when(j == 0)
    def _():
        acc_ref[...] = part

    @pl.when(j > 0)
    def _():
        acc_ref[...] += part

    @pl.when(j == pl.num_programs(1) - 1)
    def _():
        y = ALPHA * x_ref[...] + 0.5 * acc_ref[...]
        o_ref[...] = _layer_norm(y, g_ref[...], b_ref[...])


def ffn_ln(x, w_in_bf, w_out_bf, g, b, *, tm, tf):
    rows, d = x.shape
    d_ff = w_out_bf.shape[0]
    nf = d_ff // tf
    return pl.pallas_call(
        _ffn_ln_kernel,
        grid=(rows // tm, nf),
        in_specs=[
            pl.BlockSpec((tm, d), lambda i, j: (i, 0)),
            pl.BlockSpec((d, tf), lambda i, j: (0, j)),
            pl.BlockSpec((d, tf), lambda i, j, nf=nf: (0, j + nf)),
            pl.BlockSpec((tf, d), lambda i, j: (j, 0)),
            pl.BlockSpec((1, d), lambda i, j: (0, 0)),
            pl.BlockSpec((1, d), lambda i, j: (0, 0)),
        ],
        out_specs=pl.BlockSpec((tm, d), lambda i, j: (i, 0)),
        out_shape=jax.ShapeDtypeStruct((rows, d), F32),
        scratch_shapes=[pltpu.VMEM((tm, d), F32)],
        compiler_params=pltpu.CompilerParams(
            dimension_semantics=("parallel", "arbitrary"), vmem_limit_bytes=VMEM_LIMIT),
        name="ffn_ln",
    )(x, w_in_bf, w_in_bf, w_out_bf, g.reshape(1, d), b.reshape(1, d))


def _in_proj_kernel(x_ref, wa_ref, wkw_ref, wk2_ref, wg_ref, bg_ref,
                    u_ref, qb_ref, k_ref, v_ref, kb_ref, vb_ref, qib_ref, kiwi_ref, ki2_ref, gs_ref):
    xb = x_ref[...].astype(BF16)
    glu = _dot(xb, wa_ref[:, 0:C_Q])
    u_ref[...] = glu[:, :D_CONV] * _sigmoid(glu[:, D_CONV:])
    q = _dot(xb, wa_ref[:, C_Q:C_K])
    qb_ref[...] = (q * (HEAD_DIM ** -0.5)).astype(BF16)
    k = _dot(xb, wa_ref[:, C_K:C_V])
    k_ref[...] = k
    kb_ref[...] = k.astype(BF16)
    v = _dot(xb, wa_ref[:, C_V:C_QI])
    v_ref[...] = v
    vb_ref[...] = v.astype(BF16)
    qib_ref[...] = _dot(xb, wa_ref[:, C_QI:C_KI]).astype(BF16)
    kiwi_ref[...] = _dot(xb, wkw_ref[...])
    ki2_ref[...] = _dot(xb, wk2_ref[...]).astype(BF16)
    gs_ref[...] = _sigmoid(_dot(xb, wg_ref[...]) + bg_ref[...])


def in_proj(x1, wa, wkw, wk2, wg, b_gate, *, tm):
    rows, d = x1.shape
    row_spec = lambda w: pl.BlockSpec((tm, w), lambda i: (i, 0))
    full = lambda a: pl.BlockSpec(a.shape, lambda i: (0, 0))
    bg = b_gate.reshape(1, -1)
    outs = [
        (D_CONV, F32),
        (ATT_W, BF16),
        (ATT_W, F32),
        (ATT_W, F32),
        (ATT_W, BF16),
        (ATT_W, BF16),
        (ATT_W, BF16),
        (128, F32),
        (128, BF16),
        (2 * D_MODEL, F32),
    ]
    return pl.pallas_call(
        _in_proj_kernel,
        grid=(rows // tm,),
        in_specs=[row_spec(d), full(wa), full(wkw), full(wk2), full(wg), full(bg)],
        out_specs=[row_spec(w) for w, _ in outs],
        out_shape=[jax.ShapeDtypeStruct((rows, w), dt) for w, dt in outs],
        compiler_params=pltpu.CompilerParams(
            dimension_semantics=("parallel",), vmem_limit_bytes=VMEM_LIMIT),
        name="in_proj",
    )(x1, wa, wkw, wk2, wg, bg)


CONV_HALO = 32
CONV_SUB = 64


def _conv_post(c, cb_ref, g_ref, b_ref):
    y = _layer_norm(c + cb_ref[...], g_ref[...], b_ref[...])
    return y * _sigmoid(y)


def _conv_prompt_kernel(halo_ref, cur_ref, w_ref, cb_ref, g_ref, b_ref, o_ref, ext_ref, *, tm):
    i = pl.program_id(0)

    @pl.when(i == 0)
    def _():
        ext_ref[0:CONV_HALO, :] = jnp.zeros((CONV_HALO, D_CONV), F32)

    @pl.when(i > 0)
    def _():
        ext_ref[0:CONV_HALO, :] = halo_ref[...]

    ext_ref[CONV_HALO:, :] = cur_ref[...]
    off = CONV_HALO - (CONV_W - 1)
    for r0 in range(0, tm, CONV_SUB):
        acc = jnp.zeros((CONV_SUB, D_CONV), F32)
        for j in range(CONV_W):
            acc = acc + ext_ref[r0 + off + j:r0 + off + j + CONV_SUB, :] * w_ref[j:j + 1, :]
        o_ref[r0:r0 + CONV_SUB, :] = _conv_post(acc, cb_ref, g_ref, b_ref)


def conv_prompt(u, conv_w, conv_b, g, b, *, rows, tm):
    per = tm // CONV_HALO
    vec = lambda a: a.reshape(1, D_CONV)
    full = lambda shape: pl.BlockSpec(shape, lambda i: (0, 0))
    return pl.pallas_call(
        functools.partial(_conv_prompt_kernel, tm=tm),
        grid=(rows // tm,),
        in_specs=[
            pl.BlockSpec((CONV_HALO, D_CONV), lambda i: (jnp.maximum(i * per - 1, 0), 0)),
            pl.BlockSpec((tm, D_CONV), lambda i: (i, 0)),
            full((CONV_W, D_CONV)), full((1, D_CONV)), full((1, D_CONV)), full((1, D_CONV)),
        ],
        out_specs=pl.BlockSpec((tm, D_CONV), lambda i: (i, 0)),
        out_shape=jax.ShapeDtypeStruct((rows, D_CONV), F32),
        scratch_shapes=[pltpu.VMEM((CONV_HALO + tm, D_CONV), F32)],
        compiler_params=pltpu.CompilerParams(dimension_semantics=("arbitrary",)),
        name="conv_prompt",
    )(u, u, conv_w, vec(conv_b), vec(g), vec(b))


def _conv_sample_kernel(state_ref, us_ref, w_ref, cb_ref, g_ref, b_ref, o_ref):
    acc = us_ref[...] * w_ref[CONV_W - 1:CONV_W, :]
    for j in range(CONV_W - 1):
        acc = acc + state_ref[j] * w_ref[j:j + 1, :]
    o_ref[...] = _conv_post(acc, cb_ref, g_ref, b_ref)


def conv_sample(state_t, u, conv_w, conv_b, g, b, *, row0):
    nb = state_t.shape[1]
    vec = lambda a: a.reshape(1, D_CONV)
    full = lambda shape: pl.BlockSpec(shape, lambda i: (0,) * len(shape))
    assert row0 % nb == 0
    return pl.pallas_call(
        _conv_sample_kernel,
        grid=(1,),
        in_specs=[
            full(state_t.shape),
            pl.BlockSpec((nb, D_CONV), lambda i: (row0 // nb, 0)),
            full((CONV_W, D_CONV)), full((1, D_CONV)), full((1, D_CONV)), full((1, D_CONV)),
        ],
        out_specs=full((nb, D_CONV)),
        out_shape=jax.ShapeDtypeStruct((nb, D_CONV), F32),
        name="conv_sample",
    )(state_t, u, conv_w, vec(conv_b), vec(g), vec(b))


KC = 8 * ROW_TILE
KSUB = 2 * ROW_TILE
MAX_BISECTIONS = 1200
NEG_INF = float("-inf")
POS_INF = float("inf")


def _lane_tiles(x):
    return [x[:, t * ROW_TILE:(t + 1) * ROW_TILE] for t in range(x.shape[1] // ROW_TILE)]


def _fold(x, op=jnp.add):
    return functools.reduce(op, _lane_tiles(x))


def _count_rows(sc_ref, nch, pred):
    def body(c, acc):
        hits = [jnp.where(pred(tile, c * KC + t * ROW_TILE), 1.0, 0.0)
                for t, tile in enumerate(_lane_tiles(sc_ref[c]))]
        return acc + functools.reduce(jnp.add, hits)

    acc = lax.fori_loop(0, nch, body, jnp.zeros((ROW_TILE, ROW_TILE), F32))
    return jnp.sum(acc, axis=1, keepdims=True)


def _last_kept_position(tied_before, need, n_pos):
    nbits = (n_pos - 1).bit_length()

    def bit_body(b, j):
        cand = j | lax.shift_left(jnp.int32(1), nbits - 1 - b)
        return jnp.where(tied_before(cand) < need, cand, j)

    return lax.fori_loop(0, nbits, bit_body, jnp.zeros((ROW_TILE, 1), jnp.int32))


def _stats_init(stat_ref):
    stat_ref[0] = jnp.full((ROW_TILE, ROW_TILE), POS_INF, F32)
    stat_ref[1] = jnp.full((ROW_TILE, ROW_TILE), NEG_INF, F32)
    stat_ref[2] = jnp.zeros((ROW_TILE, ROW_TILE), F32)
    stat_ref[3] = jnp.zeros((ROW_TILE, ROW_TILE), F32)


def _stats_update(stat_ref, sc):
    seen = jnp.where(sc == NEG_INF, POS_INF, sc)
    stat_ref[0] = jnp.minimum(stat_ref[0], _fold(seen, jnp.minimum))
    stat_ref[1] = jnp.maximum(stat_ref[1], _fold(sc, jnp.maximum))
    stat_ref[2] += _fold(jnp.where(sc >= 0.0, 1.0, 0.0))
    stat_ref[3] += _fold(jnp.where(sc > 0.0, 1.0, 0.0))


def _select_topk(sc_ref, stat_ref, nch, n_vis, topk):
    kf = float(topk)
    wide = lambda v: jnp.broadcast_to(v, (ROW_TILE, ROW_TILE))
    rmin = jnp.min(stat_ref[0], axis=1, keepdims=True)
    rmax = jnp.max(stat_ref[1], axis=1, keepdims=True)
    cnt_ge0 = jnp.sum(stat_ref[2], axis=1, keepdims=True)
    cnt_gt0 = jnp.sum(stat_ref[3], axis=1, keepdims=True)
    needs = n_vis > kf

    def any_row(flag):
        return jnp.max(jnp.where(flag, 1.0, 0.0))

    above = cnt_gt0 >= kf
    below = cnt_ge0 < kf
    zero_tied = needs & jnp.logical_not(above) & jnp.logical_not(below)

    def smallest_positive():
        def body(c, acc):
            sc = sc_ref[c]
            return jnp.minimum(acc, _fold(jnp.where(sc > 0.0, sc, POS_INF), jnp.minimum))

        acc = lax.fori_loop(0, nch, body, jnp.full((ROW_TILE, ROW_TILE), POS_INF, F32))
        return jnp.min(acc, axis=1, keepdims=True)

    min_pos = lax.cond(any_row(zero_tied) > 0.0, smallest_positive,
                       lambda: jnp.full((ROW_TILE, 1), POS_INF, F32))

    lo0 = jnp.where(below, rmin, 0.0)
    cnt_lo0 = jnp.where(below, n_vis, cnt_ge0)
    hi_top = rmax + (jnp.abs(rmax) * 2.0 ** -10 + 1e-30)
    hi0 = jnp.where(above, hi_top, jnp.where(zero_tied, min_pos, 0.0))
    cnt_hi0 = jnp.where(above, 0.0, jnp.where(zero_tied, cnt_gt0, cnt_ge0))
    search = jnp.where(needs & jnp.logical_not(zero_tied), 1.0, 0.0)

    def probe(turn, lo, hi, cnt_lo, cnt_hi):
        half = lo + 0.5 * (hi - lo)
        frac = (jnp.log(cnt_lo) - math.log(kf)) / (jnp.log(cnt_lo) - jnp.log(jnp.maximum(cnt_hi, 0.5)))
        guess = lo + jnp.clip(frac, 0.03, 0.97) * (hi - lo)
        mid = jnp.where((turn < 2.0) & (guess > lo) & (guess < hi), guess, half)
        return mid, jnp.where((cnt_lo != kf) & (half > lo) & (half < hi), search, 0.0)

    def cond(st):
        return (st[0] < MAX_BISECTIONS) & (st[1] > 0.0)

    def body(st):
        it, _, turn, lo, hi, cnt_lo, cnt_hi, mid, open_f = st
        mid_w = wide(mid)
        cnt = _count_rows(sc_ref, nch, lambda tile, k0: tile >= mid_w)
        up = (open_f > 0.0) & (cnt >= kf)
        dn = (open_f > 0.0) & (cnt < kf)
        lo = jnp.where(up, mid, lo)
        cnt_lo = jnp.where(up, cnt, cnt_lo)
        hi = jnp.where(dn, mid, hi)
        cnt_hi = jnp.where(dn, cnt, cnt_hi)
        turn = jnp.where(turn < 2.0, turn + 1.0, 0.0)
        mid, open_f = probe(turn, lo, hi, cnt_lo, cnt_hi)
        return it + 1, jnp.max(open_f), turn, lo, hi, cnt_lo, cnt_hi, mid, open_f

    turn0 = jnp.zeros((ROW_TILE, 1), F32)
    mid0, open0 = probe(turn0, lo0, hi0, cnt_lo0, cnt_hi0)
    _, _, _, lo, hi, cnt_lo, cnt_hi, _, _ = lax.while_loop(
        cond, body, (jnp.int32(0), jnp.max(open0), turn0, lo0, hi0, cnt_lo0, cnt_hi0, mid0, open0))
    thr = jnp.where(needs, lo, rmin)

    surplus = needs & (cnt_lo > kf)

    @pl.when(any_row(surplus) > 0.0)
    def _():
        lo_w, hi_w, surplus_w = wide(lo), wide(hi), wide(surplus)
        lane = lax.broadcasted_iota(jnp.int32, (ROW_TILE, ROW_TILE), 1)
        need = kf - cnt_hi

        def tied_before(bound):
            bound_w = wide(bound)
            return _count_rows(
                sc_ref, nch, lambda tile, k0: (tile >= lo_w) & (tile < hi_w) & (lane + k0 < bound_w))

        last_w = wide(_last_kept_position(tied_before, need, sc_ref.shape[0] * KC))

        def demote(c, carry):
            kept = []
            for t, tile in enumerate(_lane_tiles(sc_ref[c])):
                pos = lane + (c * KC + t * ROW_TILE)
                drop = surplus_w & (tile >= lo_w) & (tile < hi_w) & (pos > last_w)
                kept.append(jnp.where(drop, NEG_INF, tile))
            sc_ref[c] = jnp.concatenate(kept, axis=1)
            return carry

        lax.fori_loop(0, nch, demote, 0)

    return thr


def _prompt_attn_kernel(qi_ref, kiwi_ref, q_ref, ki2_ref, k_ref, v_ref, o_ref,
                        sc_ref, stat_ref, m_ref, l_ref, acc_ref, p_ref, *, topk):
    i = pl.program_id(0)
    nch = (i * ROW_TILE) // KC + 1
    lane = lax.broadcasted_iota(jnp.int32, (ROW_TILE, ROW_TILE), 1)
    lo_half = lane < HEAD_DIM
    qpos = i * ROW_TILE + lax.broadcasted_iota(jnp.int32, (ROW_TILE, KSUB), 0)
    kiota = lax.broadcasted_iota(jnp.int32, (ROW_TILE, KSUB), 1)

    def head_lanes(ref, h):
        p = h // 2
        blk = ref[:, p * ROW_TILE:(p + 1) * ROW_TILE]
        keep = lo_half if h % 2 == 0 else jnp.logical_not(lo_half)
        return jnp.where(keep, blk, jnp.zeros_like(blk))

    qis = [head_lanes(qi_ref, h) for h in range(N_HEADS)]
    wi = kiwi_ref[...]
    wis = [jnp.broadcast_to(wi[:, D_IDX + h:D_IDX + h + 1], (ROW_TILE, ROW_TILE)) for h in range(N_HEADS)]

    def score_body(c, carry):
        for j in range(KC // KSUB):
            k0 = pl.multiple_of(c * KC, KC) + j * KSUB
            kic = ki2_ref[pl.ds(k0, KSUB), :]
            parts = [jnp.zeros((ROW_TILE, ROW_TILE), F32)] * (KSUB // ROW_TILE)
            for h in range(N_HEADS):
                s = jnp.maximum(_nt_dot(qis[h], kic), 0.0)
                parts = [acc + tile * wis[h] for acc, tile in zip(parts, _lane_tiles(s))]
            sc = jnp.where(kiota + k0 <= qpos, jnp.concatenate(parts, axis=1), NEG_INF)
            sc_ref[c, :, j * KSUB:(j + 1) * KSUB] = sc
            _stats_update(stat_ref, sc)
        return carry

    _stats_init(stat_ref)
    lax.fori_loop(0, nch, score_body, 0)

    n_vis = (qpos[:, 0:1] + 1).astype(F32)
    thr_w = jnp.broadcast_to(_select_topk(sc_ref, stat_ref, nch, n_vis, topk), (ROW_TILE, ROW_TILE))

    qs = [head_lanes(q_ref, h) for h in range(N_HEADS)]
    m_ref[...] = jnp.full(m_ref.shape, NEG_BIG, F32)

    def masked_scores(c):
        k0 = pl.multiple_of(c * KC, KC)
        bias = jnp.concatenate(
            [jnp.where(tile >= thr_w, 0.0, NEG_BIG) for tile in _lane_tiles(sc_ref[c])], axis=1)
        return k0, bias

    def max_body(c, carry):
        k0, bias = masked_scores(c)
        for p in range(N_HEADS // 2):
            kp = k_ref[pl.ds(k0, KC), p * ROW_TILE:(p + 1) * ROW_TILE]
            for h in (2 * p, 2 * p + 1):
                s = _nt_dot(qs[h], kp) + bias
                m_ref[h] = jnp.maximum(m_ref[h], _fold(s, jnp.maximum))
        return carry

    lax.fori_loop(0, nch, max_body, 0)
    for h in range(N_HEADS):
        m_ref[h] = jnp.broadcast_to(jnp.max(m_ref[h], axis=1, keepdims=True), (ROW_TILE, ROW_TILE))

    l_ref[...] = jnp.zeros(l_ref.shape, F32)
    acc_ref[...] = jnp.zeros(acc_ref.shape, F32)

    p_ref[1] = jnp.zeros(p_ref.shape[1:], BF16)

    def weighted_values(c, slot):
        k0 = pl.multiple_of(c * KC, KC)
        for p in range(N_HEADS // 2):
            vp = v_ref[pl.ds(k0, KC), p * ROW_TILE:(p + 1) * ROW_TILE]
            for h in (2 * p, 2 * p + 1):
                acc_ref[h] += _dot(p_ref[slot, h], vp)

    def pv_body(c, carry):
        slot = c % 2
        for j in range(KC // KSUB):
            k0 = pl.multiple_of(c * KC, KC) + j * KSUB
            cols = slice(j * KSUB, (j + 1) * KSUB)
            bias = jnp.concatenate(
                [jnp.where(tile >= thr_w, 0.0, NEG_BIG) for tile in _lane_tiles(sc_ref[c, :, cols])], axis=1)
            for p in range(N_HEADS // 2):
                kp = k_ref[pl.ds(k0, KSUB), p * ROW_TILE:(p + 1) * ROW_TILE]
                for h in (2 * p, 2 * p + 1):
                    s = _nt_dot(qs[h], kp) + bias
                    m = m_ref[h]
                    es = [jnp.exp(tile - m) for tile in _lane_tiles(s)]
                    l_ref[h] += functools.reduce(jnp.add, es)
                    p_ref[slot, h, :, cols] = jnp.concatenate(es, axis=1).astype(BF16)
        weighted_values(jnp.maximum(c - 1, 0), 1 - slot)
        return carry

    lax.fori_loop(0, nch, pv_body, 0)
    weighted_values(nch - 1, (nch - 1) % 2)

    for p in range(N_HEADS // 2):
        even = acc_ref[2 * p] / jnp.sum(l_ref[2 * p], axis=1, keepdims=True)
        odd = acc_ref[2 * p + 1] / jnp.sum(l_ref[2 * p + 1], axis=1, keepdims=True)
        o_ref[:, p * ROW_TILE:(p + 1) * ROW_TILE] = jnp.where(lo_half, even, odd)


def prompt_attention(qi_bf, kiwi, q_bf, ki2_bf, k_bf, v_bf, *, n_blocks, topk):
    rows = k_bf.shape[0]
    blk = lambda w: pl.BlockSpec((ROW_TILE, w), lambda i: (i, 0))
    full = lambda a: pl.BlockSpec(a.shape, lambda i: (0, 0))
    head_tiles = pltpu.VMEM((N_HEADS, ROW_TILE, ROW_TILE), F32)
    return pl.pallas_call(
        functools.partial(_prompt_attn_kernel, topk=topk),
        grid=(n_blocks,),
        in_specs=[blk(ATT_W), blk(128), blk(ATT_W), full(ki2_bf), full(k_bf), full(v_bf)],
        out_specs=blk(ATT_W),
        out_shape=jax.ShapeDtypeStruct((n_blocks * ROW_TILE, ATT_W), F32),
        scratch_shapes=[pltpu.VMEM((rows // KC, ROW_TILE, KC), F32), pltpu.VMEM((4, ROW_TILE, ROW_TILE), F32),
                        head_tiles, head_tiles, head_tiles, pltpu.VMEM((2, N_HEADS, ROW_TILE, KC), BF16)],
        compiler_params=pltpu.CompilerParams(
            dimension_semantics=("arbitrary",), vmem_limit_bytes=VMEM_LIMIT),
        name="prompt_attn",
    )(qi_bf, kiwi, q_bf, ki2_bf, k_bf, v_bf)


HEAD_ROWS = 16
PAGE_GROUP = 8
PAGE_SLOTS = 8


def _sample_index_kernel(pt_ref, qi_ref, wi_ref, kin_ref, cik_ref, sc_out, thr_out,
                         kibuf, sc_ref, stat_ref, sem, *, topk, n_pages):
    b = pl.program_id(0)

    def page_copy(pg):
        return pltpu.make_async_copy(cik_ref.at[pt_ref[b * n_pages + pg]], kibuf.at[pg], sem.at[0])

    def start(pg, carry):
        page_copy(pg).start()
        return carry

    def wait(pg, carry):
        page_copy(pg).wait()
        return carry

    lax.fori_loop(0, n_pages, start, 0)
    lax.fori_loop(0, n_pages, wait, 0)

    qi = qi_ref[0]
    wi = wi_ref[0]
    pages_per_chunk = KC // PAGE_SIZE
    for c in range(n_pages // pages_per_chunk):
        parts = []
        for t in range(pages_per_chunk):
            kit = kibuf[c * pages_per_chunk + t].astype(BF16)
            s = jnp.maximum(_dot(qi, kit), 0.0) * wi
            parts.append(jnp.sum(s, axis=0, keepdims=True))
        sc_ref[c, pl.ds(b, 1), :] = jnp.concatenate(parts, axis=1)

    kin = kin_ref[0].astype(BF16).astype(F32)
    s_new = jnp.sum(qi.astype(F32) * kin, axis=1, keepdims=True)
    sc_new = jnp.sum(jnp.maximum(s_new, 0.0) * wi, axis=0, keepdims=True)
    lane = lax.broadcasted_iota(jnp.int32, (1, KC), 1)
    new_chunk = n_pages // pages_per_chunk
    sc_ref[new_chunk, pl.ds(b, 1), :] = jnp.where(lane == 0, sc_new, NEG_INF)

    @pl.when(b == pl.num_programs(0) - 1)
    def _():
        nch = new_chunk + 1
        n_vis = jnp.full((ROW_TILE, 1), float(n_pages * PAGE_SIZE + 1), F32)
        _stats_init(stat_ref)
        for c in range(nch):
            _stats_update(stat_ref, sc_ref[c])
        thr = _select_topk(sc_ref, stat_ref, nch, n_vis, topk)
        thr_out[...] = jnp.broadcast_to(thr, (ROW_TILE, ROW_TILE))
        for c in range(nch):
            sc_out[:, c * KC:(c + 1) * KC] = sc_ref[c]


def sample_index(page_table, qi_s, wi_s, ki_new, cache_ik_t, *, topk):
    nb, n_pages = page_table.shape
    assert nb == ROW_TILE and cache_ik_t.shape[1:] == (D_IDX, PAGE_SIZE)
    nch = n_pages * PAGE_SIZE // KC + 1
    per_q = lambda shape: pl.BlockSpec((1,) + shape, lambda b, pt: (b, 0, 0))
    full = lambda shape: pl.BlockSpec(shape, lambda b, pt: (0, 0))
    grid_spec = pltpu.PrefetchScalarGridSpec(
        num_scalar_prefetch=1,
        grid=(nb,),
        in_specs=[per_q((HEAD_ROWS, D_IDX)), per_q((HEAD_ROWS, 1)), per_q((1, D_IDX)),
                  pl.BlockSpec(memory_space=pl.ANY)],
        out_specs=[full((nb, nch * KC)), full((nb, ROW_TILE))],
        scratch_shapes=[
            pltpu.VMEM((n_pages, D_IDX, PAGE_SIZE), F32),
            pltpu.VMEM((nch, ROW_TILE, KC), F32),
            pltpu.VMEM((4, ROW_TILE, ROW_TILE), F32),
            pltpu.SemaphoreType.DMA((1,)),
        ],
    )
    return pl.pallas_call(
        functools.partial(_sample_index_kernel, topk=topk, n_pages=n_pages),
        grid_spec=grid_spec,
        out_shape=[jax.ShapeDtypeStruct((nb, nch * KC), F32), jax.ShapeDtypeStruct((nb, ROW_TILE), F32)],
        compiler_params=pltpu.CompilerParams(
            dimension_semantics=("arbitrary",), vmem_limit_bytes=VMEM_LIMIT),
        name="sample_index",
    )(page_table.reshape(-1), qi_s, wi_s, ki_new, cache_ik_t)


def _sample_attn_kernel(pt_ref, bdq_ref, sc_ref, thr_ref, kn_ref, vn_ref, ck_ref, cv_ref, o_ref,
                        buf, s_ref, sem, *, n_pages):
    b = pl.program_id(0)
    past = n_pages * PAGE_SIZE
    units_per_cache = n_pages // PAGE_GROUP
    n_units = 2 * units_per_cache

    def unit_copies(q, t):
        cache = ck_ref if t < units_per_cache else cv_ref
        first = (t % units_per_cache) * PAGE_GROUP
        slot = t % PAGE_SLOTS
        return [pltpu.make_async_copy(cache.at[pt_ref[q * n_pages + first + j]], buf.at[slot, j], sem.at[slot])
                for j in range(PAGE_GROUP)]

    def start_unit(q, t):
        for cp in unit_copies(q, t):
            cp.start()

    ahead = PAGE_SLOTS - 1

    @pl.when(b == 0)
    def _():
        for t in range(ahead):
            start_unit(b, t)

    bdq = bdq_ref[0]
    out = jnp.zeros((HEAD_ROWS, ATT_W), F32)
    denom = None
    for t in range(n_units):
        for cp in unit_copies(b, t):
            cp.wait()
        if t + ahead < n_units:
            start_unit(b, t + ahead)
        else:
            @pl.when(b + 1 < pl.num_programs(0))
            def _(t=t):
                start_unit(b + 1, t + ahead - n_units)

        slot = t % PAGE_SLOTS
        for j in range(PAGE_GROUP):
            pg = (t % units_per_cache) * PAGE_GROUP + j
            lanes = slice(pg * PAGE_SIZE, (pg + 1) * PAGE_SIZE)
            page = buf[slot, j].reshape(ATT_W, PAGE_SIZE).astype(BF16)
            if t < units_per_cache:
                s_ref[:, lanes] = _dot(bdq, page)
            else:
                out = out + _nt_dot(s_ref[:, lanes].astype(BF16), page)

        if t == units_per_cache - 1:
            sc = sc_ref[0]
            thr = thr_ref[0][:, 0:1]
            sel = sc[:, :past] >= thr
            sel_new = sc[:, past:past + 1] >= thr
            s = s_ref[...] + jnp.where(sel, 0.0, NEG_BIG)
            kn = kn_ref[0].astype(BF16).astype(F32)
            s_new = jnp.sum(bdq.astype(F32) * kn, axis=1, keepdims=True)
            s_new = jnp.where(sel_new, s_new, NEG_BIG)
            m = jnp.maximum(jnp.max(s, axis=1, keepdims=True), s_new)
            pr = jnp.exp(s - m)
            p_new = jnp.where(sel_new, jnp.exp(s_new - m), 0.0)
            denom = jnp.sum(pr, axis=1, keepdims=True) + p_new
            s_ref[...] = pr
            vn = vn_ref[0].astype(BF16).astype(F32)
            out = p_new.astype(BF16).astype(F32) * vn

    out = out / denom
    head_of_lane = lax.broadcasted_iota(jnp.int32, (HEAD_ROWS, ATT_W), 1) // HEAD_DIM
    head_of_row = lax.broadcasted_iota(jnp.int32, (HEAD_ROWS, ATT_W), 0)
    o_ref[0] = jnp.sum(jnp.where(head_of_lane == head_of_row, out, 0.0), axis=0, keepdims=True)


def sample_attention(page_table, bdq, sc_all, thr, k_new, v_new, cache_k_t, cache_v_t):
    nb, n_pages = page_table.shape
    assert cache_k_t.shape[1:] == (N_HEADS, HEAD_DIM, PAGE_SIZE)
    assert (2 * n_pages // PAGE_GROUP) % PAGE_SLOTS == 0
    past = n_pages * PAGE_SIZE
    per_q = lambda shape: pl.BlockSpec((1,) + shape, lambda b, pt: (b, 0, 0))
    any_spec = pl.BlockSpec(memory_space=pl.ANY)
    grid_spec = pltpu.PrefetchScalarGridSpec(
        num_scalar_prefetch=1,
        grid=(nb,),
        in_specs=[per_q((HEAD_ROWS, ATT_W)), per_q((1, sc_all.shape[-1])), per_q((1, ROW_TILE)),
                  per_q((1, ATT_W)), per_q((1, ATT_W)), any_spec, any_spec],
        out_specs=per_q((1, ATT_W)),
        scratch_shapes=[
            pltpu.VMEM((PAGE_SLOTS, PAGE_GROUP, N_HEADS, HEAD_DIM, PAGE_SIZE), F32),
            pltpu.VMEM((HEAD_ROWS, past), F32),
            pltpu.SemaphoreType.DMA((PAGE_SLOTS,)),
        ],
    )
    return pl.pallas_call(
        functools.partial(_sample_attn_kernel, n_pages=n_pages),
        grid_spec=grid_spec,
        out_shape=jax.ShapeDtypeStruct((nb, 1, ATT_W), F32),
        compiler_params=pltpu.CompilerParams(
            dimension_semantics=("arbitrary",), vmem_limit_bytes=VMEM_LIMIT),
        name="sample_attn",
    )(page_table.reshape(-1), bdq, sc_all.reshape(nb, 1, -1), thr.reshape(nb, 1, ROW_TILE),
      k_new, v_new, cache_k_t, cache_v_t)


def _merge_ln_kernel(uc_ref, att_ref, gs_ref, x1_ref, wb0_ref, wb1_ref, wo_ref, g_ref, b_ref, o_ref):
    conv_br = _dot(uc_ref[...].astype(BF16), wb0_ref[...])
    att_br = _dot(att_ref[...].astype(BF16), wb1_ref[...])
    m = gs_ref[:, :D_MODEL] * conv_br + gs_ref[:, D_MODEL:] * att_br
    h = _dot(m.astype(BF16), wo_ref[...])
    o_ref[...] = _layer_norm(ALPHA * x1_ref[...] + h, g_ref[...], b_ref[...])


def merge_ln(u_conv, att, gs, x1, wb0, wb1, wo, g, b, *, tm):
    rows, d = x1.shape
    row_spec = lambda w: pl.BlockSpec((tm, w), lambda i: (i, 0))
    full = lambda a: pl.BlockSpec(a.shape, lambda i: (0, 0))
    g2, b2 = g.reshape(1, d), b.reshape(1, d)
    return pl.pallas_call(
        _merge_ln_kernel,
        grid=(rows // tm,),
        in_specs=[row_spec(D_CONV), row_spec(ATT_W), row_spec(2 * d), row_spec(d),
                  full(wb0), full(wb1), full(wo), full(g2), full(b2)],
        out_specs=row_spec(d),
        out_shape=jax.ShapeDtypeStruct((rows, d), F32),
        compiler_params=pltpu.CompilerParams(
            dimension_semantics=("parallel",), vmem_limit_bytes=VMEM_LIMIT),
        name="merge_ln",
    )(u_conv, att, gs, x1, wb0, wb1, wo, g2, b2)


def kernel(x_prompt, x_sample, cache_k, cache_v, cache_idx_k, state_conv, page_table, meta_tokens,
           ffn1_w_in, ffn1_w_out, ln1_g, ln1_b, w_in, conv_w, conv_b, conv_ln_g, conv_ln_b, b_gate,
           w_branch, w_out, ln2_g, ln2_b, ffn2_w_in, ffn2_w_out, ln3_g, ln3_b):
    topk_prompt = min(TOPK, SEQ // 4)
    topk_sample = min(TOPK, (PAST_LEN + 1) // 4)

    x_all = jnp.concatenate([
        meta_tokens.astype(F32), x_prompt[0],
        jnp.zeros((LP - L_PROMPT, D_MODEL), F32), x_sample[:, 0],
        jnp.zeros((R_ALL - S_ROW1, D_MODEL), F32)], axis=0)
    wa = w_in[:, :C_KI].astype(BF16)
    w_ki = w_in[:, C_KI:C_WI]
    wkw = jnp.concatenate([w_in[:, C_KI:C_G], jnp.zeros((D_MODEL, 128 - D_IDX - N_HEADS), F32)],
                          axis=1).astype(BF16)
    wk2 = jnp.concatenate([w_ki, w_ki], axis=1).astype(BF16)
    wg = w_in[:, C_G:].astype(BF16)

    x1 = ffn_ln(x_all, ffn1_w_in.astype(BF16), ffn1_w_out.astype(BF16), ln1_g, ln1_b, tm=512, tf=D_FF // 2)
    u, q_bf, k, v, k_bf, v_bf, qi_bf, kiwi, ki2_bf, gs = in_proj(x1, wa, wkw, wk2, wg, b_gate, tm=256)
    sample_rows = lambda a: a[S_ROW0:S_ROW1]
    tail_rows = jnp.zeros((R_ALL - S_ROW1, ATT_W), F32)

    uc_prompt = conv_prompt(u, conv_w, conv_b, conv_ln_g, conv_ln_b, rows=LP, tm=384)
    uc_sample = conv_sample(jnp.transpose(state_conv, (1, 0, 2)), u, conv_w, conv_b, conv_ln_g, conv_ln_b,
                            row0=S_ROW0)
    u_conv = jnp.concatenate([uc_prompt, uc_sample, tail_rows], axis=0)

    att_prompt = prompt_attention(qi_bf, kiwi, q_bf, ki2_bf, k_bf, v_bf,
                                  n_blocks=LP // ROW_TILE, topk=topk_prompt)
    pad_heads = lambda a: jnp.pad(a, ((0, 0), (0, HEAD_ROWS - N_HEADS), (0, 0)))
    kiwi_s, k_s, v_s, u_s = sample_rows(kiwi), sample_rows(k), sample_rows(v), sample_rows(u)
    qi_s = pad_heads(sample_rows(qi_bf).reshape(DEC_BATCH, N_HEADS, D_IDX))
    wi_s = pad_heads(kiwi_s[:, D_IDX:D_IDX + N_HEADS].reshape(DEC_BATCH, N_HEADS, 1))
    head_of_lane = jnp.arange(ATT_W) // HEAD_DIM
    bdq = jnp.where(head_of_lane[None, None, :] == jnp.arange(HEAD_ROWS)[None, :, None],
                    sample_rows(q_bf)[:, None, :], jnp.zeros((), BF16))
    cache_ik_t = jnp.transpose(cache_idx_k, (0, 2, 1))
    cache_k_t = jnp.transpose(cache_k, (0, 2, 3, 1))
    cache_v_t = jnp.transpose(cache_v, (0, 2, 3, 1))
    sc_all, thr = sample_index(page_table, qi_s, wi_s, kiwi_s[:, :D_IDX].reshape(DEC_BATCH, 1, D_IDX),
                               cache_ik_t, topk=topk_sample)
    att_sample = sample_attention(
        page_table, bdq, sc_all, thr,
        k_s.reshape(DEC_BATCH, 1, ATT_W), v_s.reshape(DEC_BATCH, 1, ATT_W), cache_k_t, cache_v_t)
    att = jnp.concatenate([att_prompt, att_sample[:, 0], tail_rows], axis=0)

    x2 = merge_ln(u_conv, att, gs, x1, w_branch[0].astype(BF16), w_branch[1].astype(BF16),
                  w_out.astype(BF16), ln2_g, ln2_b, tm=512)
    y = ffn_ln(x2, ffn2_w_in.astype(BF16), ffn2_w_out.astype(BF16), ln3_g, ln3_b, tm=512, tf=D_FF // 2)

    hd = (N_HEADS, HEAD_DIM)
    y_prompt = y[N_META:L_PROMPT][None]
    y_sample = sample_rows(y)[:, None]
    k_prompt = k[:L_PROMPT].reshape((1, L_PROMPT) + hd)
    v_prompt = v[:L_PROMPT].reshape((1, L_PROMPT) + hd)
    idxk_prompt = kiwi[:L_PROMPT, :D_IDX][None]
    conv_prompt_state = u[L_PROMPT - (CONV_W - 1):L_PROMPT][None]
    k_sample = k_s.reshape((DEC_BATCH, 1) + hd)
    v_sample = v_s.reshape((DEC_BATCH, 1) + hd)
    idxk_sample = kiwi_s[:, :D_IDX][:, None]
    conv_sample_state = jnp.concatenate([state_conv[:, 1:].astype(F32), u_s[:, None]], axis=1)
    return (y_prompt, y_sample, k_prompt, v_prompt, idxk_prompt, conv_prompt_state,
            k_sample, v_sample, idxk_sample, conv_sample_state)
```

```python
import functools
import math

import jax
import jax.numpy as jnp
from jax import lax
from jax.experimental import pallas as pl
from jax.experimental.pallas import tpu as pltpu

D_MODEL = 1024
SEQ = 16384
N_META = 16
L_PROMPT = SEQ + N_META
DEC_BATCH = 128
PAST_LEN = 8192
PAGE_SIZE = 128
N_PAGES = PAST_LEN // PAGE_SIZE
D_CONV = 512
CONV_W = 31
N_HEADS = 8
HEAD_DIM = 64
ATT_W = N_HEADS * HEAD_DIM
D_IDX = 64
TOPK = 256
D_FF = 2816
ALPHA = 2.0 ** 0.25
LN_EPS = 1e-5

ROW_TILE = 128
LP = 16512
S_ROW0 = LP
S_ROW1 = S_ROW0 + DEC_BATCH
R_ALL = 17408

C_Q = 2 * D_CONV
C_K = C_Q + ATT_W
C_V = C_K + ATT_W
C_QI = C_V + ATT_W
C_KI = C_QI + N_HEADS * D_IDX
C_WI = C_KI + D_IDX
C_G = C_WI + N_HEADS

NEG_BIG = -1e30
VMEM_LIMIT = 60 * 1024 * 1024

BF16 = jnp.bfloat16
F32 = jnp.float32


def _nt_dot(a, b):
    return lax.dot_general(a, b, (((1,), (1,)), ((), ())), preferred_element_type=F32)


def _dot(a, b):
    return jnp.dot(a, b, preferred_element_type=F32)


def _sigmoid(x):
    return 1.0 / (1.0 + jnp.exp(-x))


def _layer_norm(x, g, b):
    mu = jnp.mean(x, axis=-1, keepdims=True)
    xc = x - mu
    var = jnp.mean(xc * xc, axis=-1, keepdims=True)
    return xc * lax.rsqrt(var + LN_EPS) * g + b


def _ffn_ln_kernel(x_ref, wg_ref, wu_ref, wo_ref, g_ref, b_ref, o_ref, acc_ref):
    j = pl.program_id(1)
    xb = x_ref[...].astype(BF16)
    gate = _dot(xb, wg_ref[...])
    up = _dot(xb, wu_ref[...])
    h = (gate * _sigmoid(gate)) * up
    part = _dot(h.astype(BF16), wo_ref[...])

    @pl.when(j == 0)
    def _():
        acc_ref[...] = part

    @pl.when(j > 0)
    def _():
        acc_ref[...] += part

    @pl.when(j == pl.num_programs(1) - 1)
    def _():
        y = ALPHA * x_ref[...] + 0.5 * acc_ref[...]
        o_ref[...] = _layer_norm(y, g_ref[...], b_ref[...])


def ffn_ln(x, w_in_bf, w_out_bf, g, b, *, tm, tf):
    rows, d = x.shape
    d_ff = w_out_bf.shape[0]
    nf = d_ff // tf
    return pl.pallas_call(
        _ffn_ln_kernel,
        grid=(rows // tm, nf),
        in_specs=[
            pl.BlockSpec((tm, d), lambda i, j: (i, 0)),
            pl.BlockSpec((d, tf), lambda i, j: (0, j)),
            pl.BlockSpec((d, tf), lambda i, j, nf=nf: (0, j + nf)),
            pl.BlockSpec((tf, d), lambda i, j: (j, 0)),
            pl.BlockSpec((1, d), lambda i, j: (0, 0)),
            pl.BlockSpec((1, d), lambda i, j: (0, 0)),
        ],
        out_specs=pl.BlockSpec((tm, d), lambda i, j: (i, 0)),
        out_shape=jax.ShapeDtypeStruct((rows, d), F32),
        scratch_shapes=[pltpu.VMEM((tm, d), F32)],
        compiler_params=pltpu.CompilerParams(
            dimension_semantics=("parallel", "arbitrary"), vmem_limit_bytes=VMEM_LIMIT),
        name="ffn_ln",
    )(x, w_in_bf, w_in_bf, w_out_bf, g.reshape(1, d), b.reshape(1, d))


def _in_proj_kernel(x_ref, wa_ref, wkw_ref, wk2_ref, wg_ref, bg_ref,
                    u_ref, qb_ref, k_ref, v_ref, kb_ref, vb_ref, qib_ref, kiwi_ref, ki2_ref, gs_ref):
    xb = x_ref[...].astype(BF16)
    glu = _dot(xb, wa_ref[:, 0:C_Q])
    u_ref[...] = glu[:, :D_CONV] * _sigmoid(glu[:, D_CONV:])
    q = _dot(xb, wa_ref[:, C_Q:C_K])
    qb_ref[...] = (q * (HEAD_DIM ** -0.5)).astype(BF16)
    k = _dot(xb, wa_ref[:, C_K:C_V])
    k_ref[...] = k
    kb_ref[...] = k.astype(BF16)
    v = _dot(xb, wa_ref[:, C_V:C_QI])
    v_ref[...] = v
    vb_ref[...] = v.astype(BF16)
    qib_ref[...] = _dot(xb, wa_ref[:, C_QI:C_KI]).astype(BF16)
    kiwi_ref[...] = _dot(xb, wkw_ref[...])
    ki2_ref[...] = _dot(xb, wk2_ref[...]).astype(BF16)
    gs_ref[...] = _sigmoid(_dot(xb, wg_ref[...]) + bg_ref[...])


def in_proj(x1, wa, wkw, wk2, wg, b_gate, *, tm):
    rows, d = x1.shape
    row_spec = lambda w: pl.BlockSpec((tm, w), lambda i: (i, 0))
    full = lambda a: pl.BlockSpec(a.shape, lambda i: (0, 0))
    bg = b_gate.reshape(1, -1)
    outs = [
        (D_CONV, F32),
        (ATT_W, BF16),
        (ATT_W, F32),
        (ATT_W, F32),
        (ATT_W, BF16),
        (ATT_W, BF16),
        (ATT_W, BF16),
        (128, F32),
        (128, BF16),
        (2 * D_MODEL, F32),
    ]
    return pl.pallas_call(
        _in_proj_kernel,
        grid=(rows // tm,),
        in_specs=[row_spec(d), full(wa), full(wkw), full(wk2), full(wg), full(bg)],
        out_specs=[row_spec(w) for w, _ in outs],
        out_shape=[jax.ShapeDtypeStruct((rows, w), dt) for w, dt in outs],
        compiler_params=pltpu.CompilerParams(
            dimension_semantics=("parallel",), vmem_limit_bytes=VMEM_LIMIT),
        name="in_proj",
    )(x1, wa, wkw, wk2, wg, bg)


CONV_HALO = 32
CONV_SUB = 64


def _conv_post(c, cb_ref, g_ref, b_ref):
    y = _layer_norm(c + cb_ref[...], g_ref[...], b_ref[...])
    return y * _sigmoid(y)


def _conv_prompt_kernel(halo_ref, cur_ref, w_ref, cb_ref, g_ref, b_ref, o_ref, ext_ref, *, tm):
    i = pl.program_id(0)

    @pl.when(i == 0)
    def _():
        ext_ref[0:CONV_HALO, :] = jnp.zeros((CONV_HALO, D_CONV), F32)

    @pl.when(i > 0)
    def _():
        ext_ref[0:CONV_HALO, :] = halo_ref[...]

    ext_ref[CONV_HALO:, :] = cur_ref[...]
    off = CONV_HALO - (CONV_W - 1)
    for r0 in range(0, tm, CONV_SUB):
        acc = jnp.zeros((CONV_SUB, D_CONV), F32)
        for j in range(CONV_W):
            acc = acc + ext_ref[r0 + off + j:r0 + off + j + CONV_SUB, :] * w_ref[j:j + 1, :]
        o_ref[r0:r0 + CONV_SUB, :] = _conv_post(acc, cb_ref, g_ref, b_ref)


def conv_prompt(u, conv_w, conv_b, g, b, *, rows, tm):
    per = tm // CONV_HALO
    vec = lambda a: a.reshape(1, D_CONV)
    full = lambda shape: pl.BlockSpec(shape, lambda i: (0, 0))
    return pl.pallas_call(
        functools.partial(_conv_prompt_kernel, tm=tm),
        grid=(rows // tm,),
        in_specs=[
            pl.BlockSpec((CONV_HALO, D_CONV), lambda i: (jnp.maximum(i * per - 1, 0), 0)),
            pl.BlockSpec((tm, D_CONV), lambda i: (i, 0)),
            full((CONV_W, D_CONV)), full((1, D_CONV)), full((1, D_CONV)), full((1, D_CONV)),
        ],
        out_specs=pl.BlockSpec((tm, D_CONV), lambda i: (i, 0)),
        out_shape=jax.ShapeDtypeStruct((rows, D_CONV), F32),
        scratch_shapes=[pltpu.VMEM((CONV_HALO + tm, D_CONV), F32)],
        compiler_params=pltpu.CompilerParams(dimension_semantics=("arbitrary",)),
        name="conv_prompt",
    )(u, u, conv_w, vec(conv_b), vec(g), vec(b))


def _conv_sample_kernel(state_ref, us_ref, w_ref, cb_ref, g_ref, b_ref, o_ref):
    acc = us_ref[...] * w_ref[CONV_W - 1:CONV_W, :]
    for j in range(CONV_W - 1):
        acc = acc + state_ref[j] * w_ref[j:j + 1, :]
    o_ref[...] = _conv_post(acc, cb_ref, g_ref, b_ref)


def conv_sample(state_t, u, conv_w, conv_b, g, b, *, row0):
    nb = state_t.shape[1]
    vec = lambda a: a.reshape(1, D_CONV)
    full = lambda shape: pl.BlockSpec(shape, lambda i: (0,) * len(shape))
    assert row0 % nb == 0
    return pl.pallas_call(
        _conv_sample_kernel,
        grid=(1,),
        in_specs=[
            full(state_t.shape),
            pl.BlockSpec((nb, D_CONV), lambda i: (row0 // nb, 0)),
            full((CONV_W, D_CONV)), full((1, D_CONV)), full((1, D_CONV)), full((1, D_CONV)),
        ],
        out_specs=full((nb, D_CONV)),
        out_shape=jax.ShapeDtypeStruct((nb, D_CONV), F32),
        name="conv_sample",
    )(state_t, u, conv_w, vec(conv_b), vec(g), vec(b))


KC = 8 * ROW_TILE
KSUB = 2 * ROW_TILE
MAX_BISECTIONS = 1200
NEG_INF = float("-inf")
POS_INF = float("inf")


def _lane_tiles(x):
    return [x[:, t * ROW_TILE:(t + 1) * ROW_TILE] for t in range(x.shape[1] // ROW_TILE)]


def _fold(x, op=jnp.add):
    return functools.reduce(op, _lane_tiles(x))


def _count_rows(sc_ref, nch, pred):
    def body(c, acc):
        hits = [jnp.where(pred(tile, c * KC + t * ROW_TILE), 1.0, 0.0)
                for t, tile in enumerate(_lane_tiles(sc_ref[c]))]
        return acc + functools.reduce(jnp.add, hits)

    acc = lax.fori_loop(0, nch, body, jnp.zeros((ROW_TILE, ROW_TILE), F32))
    return jnp.sum(acc, axis=1, keepdims=True)


def _last_kept_position(tied_before, need, n_pos):
    nbits = (n_pos - 1).bit_length()

    def bit_body(b, j):
        cand = j | lax.shift_left(jnp.int32(1), nbits - 1 - b)
        return jnp.where(tied_before(cand) < need, cand, j)

    return lax.fori_loop(0, nbits, bit_body, jnp.zeros((ROW_TILE, 1), jnp.int32))


def _stats_init(stat_ref):
    stat_ref[0] = jnp.full((ROW_TILE, ROW_TILE), POS_INF, F32)
    stat_ref[1] = jnp.full((ROW_TILE, ROW_TILE), NEG_INF, F32)
    stat_ref[2] = jnp.zeros((ROW_TILE, ROW_TILE), F32)
    stat_ref[3] = jnp.zeros((ROW_TILE, ROW_TILE), F32)


def _stats_update(stat_ref, sc):
    seen = jnp.where(sc == NEG_INF, POS_INF, sc)
    stat_ref[0] = jnp.minimum(stat_ref[0], _fold(seen, jnp.minimum))
    stat_ref[1] = jnp.maximum(stat_ref[1], _fold(sc, jnp.maximum))
    stat_ref[2] += _fold(jnp.where(sc >= 0.0, 1.0, 0.0))
    stat_ref[3] += _fold(jnp.where(sc > 0.0, 1.0, 0.0))


def _select_topk(sc_ref, stat_ref, nch, n_vis, topk):
    kf = float(topk)
    wide = lambda v: jnp.broadcast_to(v, (ROW_TILE, ROW_TILE))
    rmin = jnp.min(stat_ref[0], axis=1, keepdims=True)
    rmax = jnp.max(stat_ref[1], axis=1, keepdims=True)
    cnt_ge0 = jnp.sum(stat_ref[2], axis=1, keepdims=True)
    cnt_gt0 = jnp.sum(stat_ref[3], axis=1, keepdims=True)
    needs = n_vis > kf

    def any_row(flag):
        return jnp.max(jnp.where(flag, 1.0, 0.0))

    above = cnt_gt0 >= kf
    below = cnt_ge0 < kf
    zero_tied = needs & jnp.logical_not(above) & jnp.logical_not(below)

    def smallest_positive():
        def body(c, acc):
            sc = sc_ref[c]
            return jnp.minimum(acc, _fold(jnp.where(sc > 0.0, sc, POS_INF), jnp.minimum))

        acc = lax.fori_loop(0, nch, body, jnp.full((ROW_TILE, ROW_TILE), POS_INF, F32))
        return jnp.min(acc, axis=1, keepdims=True)

    min_pos = lax.cond(any_row(zero_tied) > 0.0, smallest_positive,
                       lambda: jnp.full((ROW_TILE, 1), POS_INF, F32))

    lo0 = jnp.where(below, rmin, 0.0)
    cnt_lo0 = jnp.where(below, n_vis, cnt_ge0)
    hi_top = rmax + (jnp.abs(rmax) * 2.0 ** -10 + 1e-30)
    hi0 = jnp.where(above, hi_top, jnp.where(zero_tied, min_pos, 0.0))
    cnt_hi0 = jnp.where(above, 0.0, jnp.where(zero_tied, cnt_gt0, cnt_ge0))
    search = jnp.where(needs & jnp.logical_not(zero_tied), 1.0, 0.0)

    def probe(turn, lo, hi, cnt_lo, cnt_hi):
        half = lo + 0.5 * (hi - lo)
        frac = (jnp.log(cnt_lo) - math.log(kf)) / (jnp.log(cnt_lo) - jnp.log(jnp.maximum(cnt_hi, 0.5)))
        guess = lo + jnp.clip(frac, 0.03, 0.97) * (hi - lo)
        mid = jnp.where((turn < 2.0) & (guess > lo) & (guess < hi), guess, half)
        return mid, jnp.where((cnt_lo != kf) & (half > lo) & (half < hi), search, 0.0)

    def cond(st):
        return (st[0] < MAX_BISECTIONS) & (st[1] > 0.0)

    def body(st):
        it, _, turn, lo, hi, cnt_lo, cnt_hi, mid, open_f = st
        mid_w = wide(mid)
        cnt = _count_rows(sc_ref, nch, lambda tile, k0: tile >= mid_w)
        up = (open_f > 0.0) & (cnt >= kf)
        dn = (open_f > 0.0) & (cnt < kf)
        lo = jnp.where(up, mid, lo)
        cnt_lo = jnp.where(up, cnt, cnt_lo)
        hi = jnp.where(dn, mid, hi)
        cnt_hi = jnp.where(dn, cnt, cnt_hi)
        turn = jnp.where(turn < 2.0, turn + 1.0, 0.0)
        mid, open_f = probe(turn, lo, hi, cnt_lo, cnt_hi)
        return it + 1, jnp.max(open_f), turn, lo, hi, cnt_lo, cnt_hi, mid, open_f

    turn0 = jnp.zeros((ROW_TILE, 1), F32)
    mid0, open0 = probe(turn0, lo0, hi0, cnt_lo0, cnt_hi0)
    _, _, _, lo, hi, cnt_lo, cnt_hi, _, _ = lax.while_loop(
        cond, body, (jnp.int32(0), jnp.max(open0), turn0, lo0, hi0, cnt_lo0, cnt_hi0, mid0, open0))
    thr = jnp.where(needs, lo, rmin)

    surplus = needs & (cnt_lo > kf)

    @pl.when(any_row(surplus) > 0.0)
    def _():
        lo_w, hi_w, surplus_w = wide(lo), wide(hi), wide(surplus)
        lane = lax.broadcasted_iota(jnp.int32, (ROW_TILE, ROW_TILE), 1)
        need = kf - cnt_hi

        def tied_before(bound):
            bound_w = wide(bound)
            return _count_rows(
                sc_ref, nch, lambda tile, k0: (tile >= lo_w) & (tile < hi_w) & (lane + k0 < bound_w))

        last_w = wide(_last_kept_position(tied_before, need, sc_ref.shape[0] * KC))

        def demote(c, carry):
            kept = []
            for t, tile in enumerate(_lane_tiles(sc_ref[c])):
                pos = lane + (c * KC + t * ROW_TILE)
                drop = surplus_w & (tile >= lo_w) & (tile < hi_w) & (pos > last_w)
                kept.append(jnp.where(drop, NEG_INF, tile))
            sc_ref[c] = jnp.concatenate(kept, axis=1)
            return carry

        lax.fori_loop(0, nch, demote, 0)

    return thr


def _prompt_attn_kernel(qi_ref, kiwi_ref, q_ref, ki2_ref, k_ref, v_ref, o_ref,
                        sc_ref, stat_ref, m_ref, l_ref, acc_ref, p_ref, *, topk):
    i = pl.program_id(0)
    nch = (i * ROW_TILE) // KC + 1
    lane = lax.broadcasted_iota(jnp.int32, (ROW_TILE, ROW_TILE), 1)
    lo_half = lane < HEAD_DIM
    qpos = i * ROW_TILE + lax.broadcasted_iota(jnp.int32, (ROW_TILE, KSUB), 0)
    kiota = lax.broadcasted_iota(jnp.int32, (ROW_TILE, KSUB), 1)

    def head_lanes(ref, h):
        p = h // 2
        blk = ref[:, p * ROW_TILE:(p + 1) * ROW_TILE]
        keep = lo_half if h % 2 == 0 else jnp.logical_not(lo_half)
        return jnp.where(keep, blk, jnp.zeros_like(blk))

    qis = [head_lanes(qi_ref, h) for h in range(N_HEADS)]
    wi = kiwi_ref[...]
    wis = [jnp.broadcast_to(wi[:, D_IDX + h:D_IDX + h + 1], (ROW_TILE, ROW_TILE)) for h in range(N_HEADS)]

    def score_body(c, carry):
        for j in range(KC // KSUB):
            k0 = pl.multiple_of(c * KC, KC) + j * KSUB
            kic = ki2_ref[pl.ds(k0, KSUB), :]
            parts = [jnp.zeros((ROW_TILE, ROW_TILE), F32)] * (KSUB // ROW_TILE)
            for h in range(N_HEADS):
                s = jnp.maximum(_nt_dot(qis[h], kic), 0.0)
                parts = [acc + tile * wis[h] for acc, tile in zip(parts, _lane_tiles(s))]
            sc = jnp.where(kiota + k0 <= qpos, jnp.concatenate(parts, axis=1), NEG_INF)
            sc_ref[c, :, j * KSUB:(j + 1) * KSUB] = sc
            _stats_update(stat_ref, sc)
        return carry

    _stats_init(stat_ref)
    lax.fori_loop(0, nch, score_body, 0)

    n_vis = (qpos[:, 0:1] + 1).astype(F32)
    thr_w = jnp.broadcast_to(_select_topk(sc_ref, stat_ref, nch, n_vis, topk), (ROW_TILE, ROW_TILE))

    qs = [head_lanes(q_ref, h) for h in range(N_HEADS)]
    m_ref[...] = jnp.full(m_ref.shape, NEG_BIG, F32)

    def masked_scores(c):
        k0 = pl.multiple_of(c * KC, KC)
        bias = jnp.concatenate(
            [jnp.where(tile >= thr_w, 0.0, NEG_BIG) for tile in _lane_tiles(sc_ref[c])], axis=1)
        return k0, bias

    def max_body(c, carry):
        k0, bias = masked_scores(c)
        for p in range(N_HEADS // 2):
            kp = k_ref[pl.ds(k0, KC), p * ROW_TILE:(p + 1) * ROW_TILE]
            for h in (2 * p, 2 * p + 1):
                s = _nt_dot(qs[h], kp) + bias
                m_ref[h] = jnp.maximum(m_ref[h], _fold(s, jnp.maximum))
        return carry

    lax.fori_loop(0, nch, max_body, 0)
    for h in range(N_HEADS):
        m_ref[h] = jnp.broadcast_to(jnp.max(m_ref[h], axis=1, keepdims=True), (ROW_TILE, ROW_TILE))

    l_ref[...] = jnp.zeros(l_ref.shape, F32)
    acc_ref[...] = jnp.zeros(acc_ref.shape, F32)

    p_ref[1] = jnp.zeros(p_ref.shape[1:], BF16)

    def weighted_values(c, slot):
        k0 = pl.multiple_of(c * KC, KC)
        for p in range(N_HEADS // 2):
            vp = v_ref[pl.ds(k0, KC), p * ROW_TILE:(p + 1) * ROW_TILE]
            for h in (2 * p, 2 * p + 1):
                acc_ref[h] += _dot(p_ref[slot, h], vp)

    def pv_body(c, carry):
        slot = c % 2
        for j in range(KC // KSUB):
            k0 = pl.multiple_of(c * KC, KC) + j * KSUB
            cols = slice(j * KSUB, (j + 1) * KSUB)
            bias = jnp.concatenate(
                [jnp.where(tile >= thr_w, 0.0, NEG_BIG) for tile in _lane_tiles(sc_ref[c, :, cols])], axis=1)
            for p in range(N_HEADS // 2):
                kp = k_ref[pl.ds(k0, KSUB), p * ROW_TILE:(p + 1) * ROW_TILE]
                for h in (2 * p, 2 * p + 1):
                    s = _nt_dot(qs[h], kp) + bias
                    m = m_ref[h]
                    es = [jnp.exp(tile - m) for tile in _lane_tiles(s)]
                    l_ref[h] += functools.reduce(jnp.add, es)
                    p_ref[slot, h, :, cols] = jnp.concatenate(es, axis=1).astype(BF16)
        weighted_values(jnp.maximum(c - 1, 0), 1 - slot)
        return carry

    lax.fori_loop(0, nch, pv_body, 0)
    weighted_values(nch - 1, (nch - 1) % 2)

    for p in range(N_HEADS // 2):
        even = acc_ref[2 * p] / jnp.sum(l_ref[2 * p], axis=1, keepdims=True)
        odd = acc_ref[2 * p + 1] / jnp.sum(l_ref[2 * p + 1], axis=1, keepdims=True)
        o_ref[:, p * ROW_TILE:(p + 1) * ROW_TILE] = jnp.where(lo_half, even, odd)


def prompt_attention(qi_bf, kiwi, q_bf, ki2_bf, k_bf, v_bf, *, n_blocks, topk):
    rows = k_bf.shape[0]
    blk = lambda w: pl.BlockSpec((ROW_TILE, w), lambda i: (i, 0))
    full = lambda a: pl.BlockSpec(a.shape, lambda i: (0, 0))
    head_tiles = pltpu.VMEM((N_HEADS, ROW_TILE, ROW_TILE), F32)
    return pl.pallas_call(
        functools.partial(_prompt_attn_kernel, topk=topk),
        grid=(n_blocks,),
        in_specs=[blk(ATT_W), blk(128), blk(ATT_W), full(ki2_bf), full(k_bf), full(v_bf)],
        out_specs=blk(ATT_W),
        out_shape=jax.ShapeDtypeStruct((n_blocks * ROW_TILE, ATT_W), F32),
        scratch_shapes=[pltpu.VMEM((rows // KC, ROW_TILE, KC), F32), pltpu.VMEM((4, ROW_TILE, ROW_TILE), F32),
                        head_tiles, head_tiles, head_tiles, pltpu.VMEM((2, N_HEADS, ROW_TILE, KC), BF16)],
        compiler_params=pltpu.CompilerParams(
            dimension_semantics=("arbitrary",), vmem_limit_bytes=VMEM_LIMIT),
        name="prompt_attn",
    )(qi_bf, kiwi, q_bf, ki2_bf, k_bf, v_bf)


HEAD_ROWS = 16
PAGE_GROUP = 8
PAGE_SLOTS = 8


def _sample_index_kernel(pt_ref, qi_ref, wi_ref, kin_ref, cik_ref, sc_out, thr_out,
                         kibuf, sc_ref, stat_ref, sem, *, topk, n_pages):
    b = pl.program_id(0)

    def page_copy(pg):
        return pltpu.make_async_copy(cik_ref.at[pt_ref[b * n_pages + pg]], kibuf.at[pg], sem.at[0])

    def wait(pg, carry):
        page_copy(pg).wait()
        return carry

    for pg in range(n_pages):
        page_copy(pg).start(priority=pg % 2)
    lax.fori_loop(0, n_pages, wait, 0)

    qi = qi_ref[0]
    wi = wi_ref[0]
    pages_per_chunk = KC // PAGE_SIZE
    for c in range(n_pages // pages_per_chunk):
        parts = []
        for t in range(pages_per_chunk):
            kit = kibuf[c * pages_per_chunk + t].astype(BF16)
            s = jnp.maximum(_dot(qi, kit), 0.0) * wi
            parts.append(jnp.sum(s, axis=0, keepdims=True))
        sc_ref[c, pl.ds(b, 1), :] = jnp.concatenate(parts, axis=1)

    kin = kin_ref[0].astype(BF16).astype(F32)
    s_new = jnp.sum(qi.astype(F32) * kin, axis=1, keepdims=True)
    sc_new = jnp.sum(jnp.maximum(s_new, 0.0) * wi, axis=0, keepdims=True)
    lane = lax.broadcasted_iota(jnp.int32, (1, KC), 1)
    new_chunk = n_pages // pages_per_chunk
    sc_ref[new_chunk, pl.ds(b, 1), :] = jnp.where(lane == 0, sc_new, NEG_INF)

    @pl.when(b == pl.num_programs(0) - 1)
    def _():
        nch = new_chunk + 1
        n_vis = jnp.full((ROW_TILE, 1), float(n_pages * PAGE_SIZE + 1), F32)
        _stats_init(stat_ref)
        for c in range(nch):
            _stats_update(stat_ref, sc_ref[c])
        thr = _select_topk(sc_ref, stat_ref, nch, n_vis, topk)
        thr_out[...] = jnp.broadcast_to(thr, (ROW_TILE, ROW_TILE))
        for c in range(nch):
            sc_out[:, c * KC:(c + 1) * KC] = sc_ref[c]


def sample_index(page_table, qi_s, wi_s, ki_new, cache_ik_t, *, topk):
    nb, n_pages = page_table.shape
    assert nb == ROW_TILE and cache_ik_t.shape[1:] == (D_IDX, PAGE_SIZE)
    nch = n_pages * PAGE_SIZE // KC + 1
    per_q = lambda shape: pl.BlockSpec((1,) + shape, lambda b, pt: (b, 0, 0))
    full = lambda shape: pl.BlockSpec(shape, lambda b, pt: (0, 0))
    grid_spec = pltpu.PrefetchScalarGridSpec(
        num_scalar_prefetch=1,
        grid=(nb,),
        in_specs=[per_q((HEAD_ROWS, D_IDX)), per_q((HEAD_ROWS, 1)), per_q((1, D_IDX)),
                  pl.BlockSpec(memory_space=pl.ANY)],
        out_specs=[full((nb, nch * KC)), full((nb, ROW_TILE))],
        scratch_shapes=[
            pltpu.VMEM((n_pages, D_IDX, PAGE_SIZE), F32),
            pltpu.VMEM((nch, ROW_TILE, KC), F32),
            pltpu.VMEM((4, ROW_TILE, ROW_TILE), F32),
            pltpu.SemaphoreType.DMA((1,)),
        ],
    )
    return pl.pallas_call(
        functools.partial(_sample_index_kernel, topk=topk, n_pages=n_pages),
        grid_spec=grid_spec,
        out_shape=[jax.ShapeDtypeStruct((nb, nch * KC), F32), jax.ShapeDtypeStruct((nb, ROW_TILE), F32)],
        compiler_params=pltpu.CompilerParams(
            dimension_semantics=("arbitrary",), vmem_limit_bytes=VMEM_LIMIT),
        name="sample_index",
    )(page_table.reshape(-1), qi_s, wi_s, ki_new, cache_ik_t)


def _sample_attn_kernel(pt_ref, bdq_ref, sc_ref, thr_ref, kn_ref, vn_ref, ck_ref, cv_ref, o_ref,
                        buf, s_ref, sem, *, n_pages):
    b = pl.program_id(0)
    past = n_pages * PAGE_SIZE
    units_per_cache = n_pages // PAGE_GROUP
    n_units = 2 * units_per_cache

    def unit_copies(q, t):
        cache = ck_ref if t < units_per_cache else cv_ref
        first = (t % units_per_cache) * PAGE_GROUP
        slot = t % PAGE_SLOTS
        return [pltpu.make_async_copy(cache.at[pt_ref[q * n_pages + first + j]], buf.at[slot, j], sem.at[slot])
                for j in range(PAGE_GROUP)]

    def start_unit(q, t):
        for cp in unit_copies(q, t):
            cp.start()

    ahead = PAGE_SLOTS - 1

    @pl.when(b == 0)
    def _():
        for t in range(ahead):
            start_unit(b, t)

    bdq = bdq_ref[0]
    out = jnp.zeros((HEAD_ROWS, ATT_W), F32)
    denom = None
    for t in range(n_units):
        for cp in unit_copies(b, t):
            cp.wait()
        if t + ahead < n_units:
            start_unit(b, t + ahead)
        else:
            @pl.when(b + 1 < pl.num_programs(0))
            def _(t=t):
                start_unit(b + 1, t + ahead - n_units)

        slot = t % PAGE_SLOTS
        for j in range(PAGE_GROUP):
            pg = (t % units_per_cache) * PAGE_GROUP + j
            lanes = slice(pg * PAGE_SIZE, (pg + 1) * PAGE_SIZE)
            page = buf[slot, j].reshape(ATT_W, PAGE_SIZE).astype(BF16)
            if t < units_per_cache:
                s_ref[:, lanes] = _dot(bdq, page)
            else:
                out = out + _nt_dot(s_ref[:, lanes].astype(BF16), page)

        if t == units_per_cache - 1:
            sc = sc_ref[0]
            thr = thr_ref[0][:, 0:1]
            sel = sc[:, :past] >= thr
            sel_new = sc[:, past:past + 1] >= thr
            s = s_ref[...] + jnp.where(sel, 0.0, NEG_BIG)
            kn = kn_ref[0].astype(BF16).astype(F32)
            s_new = jnp.sum(bdq.astype(F32) * kn, axis=1, keepdims=True)
            s_new = jnp.where(sel_new, s_new, NEG_BIG)
            m = jnp.maximum(jnp.max(s, axis=1, keepdims=True), s_new)
            pr = jnp.exp(s - m)
            p_new = jnp.where(sel_new, jnp.exp(s_new - m), 0.0)
            denom = jnp.sum(pr, axis=1, keepdims=True) + p_new
            s_ref[...] = pr
            vn = vn_ref[0].astype(BF16).astype(F32)
            out = p_new.astype(BF16).astype(F32) * vn

    out = out / denom
    head_of_lane = lax.broadcasted_iota(jnp.int32, (HEAD_ROWS, ATT_W), 1) // HEAD_DIM
    head_of_row = lax.broadcasted_iota(jnp.int32, (HEAD_ROWS, ATT_W), 0)
    o_ref[0] = jnp.sum(jnp.where(head_of_lane == head_of_row, out, 0.0), axis=0, keepdims=True)


def sample_attention(page_table, bdq, sc_all, thr, k_new, v_new, cache_k_t, cache_v_t):
    nb, n_pages = page_table.shape
    assert cache_k_t.shape[1:] == (N_HEADS, HEAD_DIM, PAGE_SIZE)
    assert (2 * n_pages // PAGE_GROUP) % PAGE_SLOTS == 0
    past = n_pages * PAGE_SIZE
    per_q = lambda shape: pl.BlockSpec((1,) + shape, lambda b, pt: (b, 0, 0))
    any_spec = pl.BlockSpec(memory_space=pl.ANY)
    grid_spec = pltpu.PrefetchScalarGridSpec(
        num_scalar_prefetch=1,
        grid=(nb,),
        in_specs=[per_q((HEAD_ROWS, ATT_W)), per_q((1, sc_all.shape[-1])), per_q((1, ROW_TILE)),
                  per_q((1, ATT_W)), per_q((1, ATT_W)), any_spec, any_spec],
        out_specs=per_q((1, ATT_W)),
        scratch_shapes=[
            pltpu.VMEM((PAGE_SLOTS, PAGE_GROUP, N_HEADS, HEAD_DIM, PAGE_SIZE), F32),
            pltpu.VMEM((HEAD_ROWS, past), F32),
            pltpu.SemaphoreType.DMA((PAGE_SLOTS,)),
        ],
    )
    return pl.pallas_call(
        functools.partial(_sample_attn_kernel, n_pages=n_pages),
        grid_spec=grid_spec,
        out_shape=jax.ShapeDtypeStruct((nb, 1, ATT_W), F32),
        compiler_params=pltpu.CompilerParams(
            dimension_semantics=("arbitrary",), vmem_limit_bytes=VMEM_LIMIT),
        name="sample_attn",
    )(page_table.reshape(-1), bdq, sc_all.reshape(nb, 1, -1), thr.reshape(nb, 1, ROW_TILE),
      k_new, v_new, cache_k_t, cache_v_t)


def _merge_ln_kernel(uc_ref, att_ref, gs_ref, x1_ref, wb0_ref, wb1_ref, wo_ref, g_ref, b_ref, o_ref):
    conv_br = _dot(uc_ref[...].astype(BF16), wb0_ref[...])
    att_br = _dot(att_ref[...].astype(BF16), wb1_ref[...])
    m = gs_ref[:, :D_MODEL] * conv_br + gs_ref[:, D_MODEL:] * att_br
    h = _dot(m.astype(BF16), wo_ref[...])
    o_ref[...] = _layer_norm(ALPHA * x1_ref[...] + h, g_ref[...], b_ref[...])


def merge_ln(u_conv, att, gs, x1, wb0, wb1, wo, g, b, *, tm):
    rows, d = x1.shape
    row_spec = lambda w: pl.BlockSpec((tm, w), lambda i: (i, 0))
    full = lambda a: pl.BlockSpec(a.shape, lambda i: (0, 0))
    g2, b2 = g.reshape(1, d), b.reshape(1, d)
    return pl.pallas_call(
        _merge_ln_kernel,
        grid=(rows // tm,),
        in_specs=[row_spec(D_CONV), row_spec(ATT_W), row_spec(2 * d), row_spec(d),
                  full(wb0), full(wb1), full(wo), full(g2), full(b2)],
        out_specs=row_spec(d),
        out_shape=jax.ShapeDtypeStruct((rows, d), F32),
        compiler_params=pltpu.CompilerParams(
            dimension_semantics=("parallel",), vmem_limit_bytes=VMEM_LIMIT),
        name="merge_ln",
    )(u_conv, att, gs, x1, wb0, wb1, wo, g2, b2)


def kernel(x_prompt, x_sample, cache_k, cache_v, cache_idx_k, state_conv, page_table, meta_tokens,
           ffn1_w_in, ffn1_w_out, ln1_g, ln1_b, w_in, conv_w, conv_b, conv_ln_g, conv_ln_b, b_gate,
           w_branch, w_out, ln2_g, ln2_b, ffn2_w_in, ffn2_w_out, ln3_g, ln3_b):
    topk_prompt = min(TOPK, SEQ // 4)
    topk_sample = min(TOPK, (PAST_LEN + 1) // 4)

    x_all = jnp.concatenate([
        meta_tokens.astype(F32), x_prompt[0],
        jnp.zeros((LP - L_PROMPT, D_MODEL), F32), x_sample[:, 0],
        jnp.zeros((R_ALL - S_ROW1, D_MODEL), F32)], axis=0)
    wa = w_in[:, :C_KI].astype(BF16)
    w_ki = w_in[:, C_KI:C_WI]
    wkw = jnp.concatenate([w_in[:, C_KI:C_G], jnp.zeros((D_MODEL, 128 - D_IDX - N_HEADS), F32)],
                          axis=1).astype(BF16)
    wk2 = jnp.concatenate([w_ki, w_ki], axis=1).astype(BF16)
    wg = w_in[:, C_G:].astype(BF16)

    x1 = ffn_ln(x_all, ffn1_w_in.astype(BF16), ffn1_w_out.astype(BF16), ln1_g, ln1_b, tm=512, tf=D_FF // 2)
    u, q_bf, k, v, k_bf, v_bf, qi_bf, kiwi, ki2_bf, gs = in_proj(x1, wa, wkw, wk2, wg, b_gate, tm=256)
    sample_rows = lambda a: a[S_ROW0:S_ROW1]
    tail_rows = jnp.zeros((R_ALL - S_ROW1, ATT_W), F32)

    uc_prompt = conv_prompt(u, conv_w, conv_b, conv_ln_g, conv_ln_b, rows=LP, tm=384)
    uc_sample = conv_sample(jnp.transpose(state_conv, (1, 0, 2)), u, conv_w, conv_b, conv_ln_g, conv_ln_b,
                            row0=S_ROW0)
    u_conv = jnp.concatenate([uc_prompt, uc_sample, tail_rows], axis=0)

    att_prompt = prompt_attention(qi_bf, kiwi, q_bf, ki2_bf, k_bf, v_bf,
                                  n_blocks=LP // ROW_TILE, topk=topk_prompt)
    pad_heads = lambda a: jnp.pad(a, ((0, 0), (0, HEAD_ROWS - N_HEADS), (0, 0)))
    kiwi_s, k_s, v_s, u_s = sample_rows(kiwi), sample_rows(k), sample_rows(v), sample_rows(u)
    qi_s = pad_heads(sample_rows(qi_bf).reshape(DEC_BATCH, N_HEADS, D_IDX))
    wi_s = pad_heads(kiwi_s[:, D_IDX:D_IDX + N_HEADS].reshape(DEC_BATCH, N_HEADS, 1))
    head_of_lane = jnp.arange(ATT_W) // HEAD_DIM
    bdq = jnp.where(head_of_lane[None, None, :] == jnp.arange(HEAD_ROWS)[None, :, None],
                    sample_rows(q_bf)[:, None, :], jnp.zeros((), BF16))
    cache_ik_t = jnp.transpose(cache_idx_k, (0, 2, 1))
    cache_k_t = jnp.transpose(cache_k, (0, 2, 3, 1))
    cache_v_t = jnp.transpose(cache_v, (0, 2, 3, 1))
    sc_all, thr = sample_index(page_table, qi_s, wi_s, kiwi_s[:, :D_IDX].reshape(DEC_BATCH, 1, D_IDX),
                               cache_ik_t, topk=topk_sample)
    att_sample = sample_attention(
        page_table, bdq, sc_all, thr,
        k_s.reshape(DEC_BATCH, 1, ATT_W), v_s.reshape(DEC_BATCH, 1, ATT_W), cache_k_t, cache_v_t)
    att = jnp.concatenate([att_prompt, att_sample[:, 0], tail_rows], axis=0)

    x2 = merge_ln(u_conv, att, gs, x1, w_branch[0].astype(BF16), w_branch[1].astype(BF16),
                  w_out.astype(BF16), ln2_g, ln2_b, tm=512)
    y = ffn_ln(x2, ffn2_w_in.astype(BF16), ffn2_w_out.astype(BF16), ln3_g, ln3_b, tm=512, tf=D_FF // 2)

    hd = (N_HEADS, HEAD_DIM)
    y_prompt = y[N_META:L_PROMPT][None]
    y_sample = sample_rows(y)[:, None]
    k_prompt = k[:L_PROMPT].reshape((1, L_PROMPT) + hd)
    v_prompt = v[:L_PROMPT].reshape((1, L_PROMPT) + hd)
    idxk_prompt = kiwi[:L_PROMPT, :D_IDX][None]
    conv_prompt_state = u[L_PROMPT - (CONV_W - 1):L_PROMPT][None]
    k_sample = k_s.reshape((DEC_BATCH, 1) + hd)
    v_sample = v_s.reshape((DEC_BATCH, 1) + hd)
    idxk_sample = kiwi_s[:, :D_IDX][:, None]
    conv_sample_state = jnp.concatenate([state_conv[:, 1:].astype(F32), u_s[:, None]], axis=1)
    return (y_prompt, y_sample, k_prompt, v_prompt, idxk_prompt, conv_prompt_state,
            k_sample, v_sample, idxk_sample, conv_sample_state)
```
